```python
import math
import jax, jax.numpy as jnp
from jax import lax
import numpy as np

D_MODEL = 4096
BATCH = 2
SEQ = 4096
DEPTH = 2

HEAD_DIM = 128
N_HEADS = D_MODEL // HEAD_DIM
N_HEADS_DIL = N_HEADS // 2
N_HEADS_DIFF = N_HEADS - N_HEADS_DIL
DIFF_QK_DIM = HEAD_DIM // 2
DIL_WIDTH = N_HEADS_DIL * HEAD_DIM
DIFF_WIDTH = N_HEADS_DIFF * HEAD_DIM
MIX_WIDTH = DIL_WIDTH + DIFF_WIDTH
IN_PROJ_WIDTH = 3 * DIL_WIDTH + 3 * DIFF_WIDTH
DILATED_CONFIGS = ((128, 1), (512, 4), (2048, 16))
D_FF = 4 * D_MODEL
NUM_BUCKETS = 32
MAX_DISTANCE = 1024
QUERY_BLOCK = 128
RMS_EPS = 1e-6
SUBLN_EPS = 1e-5
NEG_INF = -1e30

kernel_name = "hybrid_dilated_diff_attn_encoder"


def rmsnorm(x, g, eps=RMS_EPS):
    xf = x.astype(jnp.float32)
    y = xf * lax.rsqrt(jnp.mean(xf * xf, axis=-1, keepdims=True) + eps)
    return (y * g.astype(jnp.float32)).astype(x.dtype)


def t5_bucket(rel):
    half = NUM_BUCKETS // 2
    max_exact = half // 2
    ret = jnp.where(rel > 0, half, 0)
    n = jnp.abs(rel)
    nf = jnp.maximum(n, 1).astype(jnp.float32)
    large = max_exact + (jnp.log(nf / max_exact) / math.log(MAX_DISTANCE / max_exact)
                         * (half - max_exact)).astype(jnp.int32)
    large = jnp.minimum(large, half - 1)
    return ret + jnp.where(n < max_exact, n, large)


def dilated_branch(q, k, v, table, window, dil):
    Bsz, H, S, hd = q.shape
    radius = window // (2 * dil)
    blk = radius
    n_sub = -(-S // dil)
    L = -(-n_sub // blk) * blk
    S_pad = L * dil
    nb = L // blk

    def to_sub(z):
        z = jnp.pad(z, ((0, 0), (0, 0), (0, S_pad - S), (0, 0)))
        return z.reshape(Bsz, H, L, dil, z.shape[-1]).transpose(0, 1, 3, 2, 4)

    def key_windows(z):
        zs = to_sub(z)
        zs = jnp.pad(zs, ((0, 0), (0, 0), (0, 0), (blk, blk), (0, 0)))
        zs = zs.reshape(Bsz, H, dil, nb + 2, blk, zs.shape[-1])
        return jnp.concatenate([zs[:, :, :, :-2], zs[:, :, :, 1:-1], zs[:, :, :, 2:]], axis=4)

    qs = to_sub(q).reshape(Bsz, H, dil, nb, blk, hd)
    ks = key_windows(k)
    vs = key_windows(v)

    valid = (jnp.arange(S_pad) < S).reshape(L, dil).T
    valid = jnp.pad(valid, ((0, 0), (blk, blk))).reshape(dil, nb + 2, blk)
    valid = jnp.concatenate([valid[:, :-2], valid[:, 1:-1], valid[:, 2:]], axis=2)

    off = jnp.arange(3 * blk)[None, :] - blk - jnp.arange(blk)[:, None]
    band = jnp.abs(off) <= radius
    bias = table[t5_bucket(off * dil)].astype(jnp.float32).transpose(2, 0, 1)

    s = jnp.einsum('bhrnqd,bhrnkd->bhrnqk', qs, ks).astype(jnp.float32)
    s = s + bias[None, :, None, None]
    mask = band[None, None, None, None] & valid[None, None, :, :, None, :]
    s = jnp.where(mask, s, NEG_INF)
    m = jnp.max(s, axis=-1, keepdims=True)
    p = jnp.exp(s - m)
    l = jnp.sum(p, axis=-1, keepdims=True)
    n = jnp.einsum('bhrnqk,bhrnkd->bhrnqd', p, vs.astype(jnp.float32))

    def from_sub(z):
        c = z.shape[-1]
        z = z.reshape(Bsz, H, dil, L, c).transpose(0, 1, 3, 2, 4).reshape(Bsz, H, S_pad, c)
        return z[:, :, :S]

    return from_sub(m), from_sub(l), from_sub(n)


def dilated_attention(q, k, v, table):
    outs = [dilated_branch(q, k, v, table, w, d) for (w, d) in DILATED_CONFIGS]
    m_all = outs[0][0]
    for m, _, _ in outs[1:]:
        m_all = jnp.maximum(m_all, m)
    num = 0.0
    den = 0.0
    for m, l, n in outs:
        scale = jnp.exp(m - m_all)
        num = num + scale * n
        den = den + scale * l
    return num / den


def diff_attention(q, k, v, table, lam):
    Bsz, H, _, S, dq = q.shape
    nq = S // QUERY_BLOCK
    qb = q.reshape(Bsz, H, 2, nq, QUERY_BLOCK, dq).transpose(3, 0, 1, 2, 4, 5)
    vf = v.astype(jnp.float32)
    kpos = jnp.arange(S)

    def body(args):
        qblk, bi = args
        s = jnp.einsum('bhmqd,bhmkd->bhmqk', qblk, k).astype(jnp.float32)
        qpos = bi * QUERY_BLOCK + jnp.arange(QUERY_BLOCK)
        bias = table[t5_bucket(kpos[None, :] - qpos[:, None])].astype(jnp.float32)
        s = s + bias.transpose(2, 0, 1)[None, :, None]
        p = jax.nn.softmax(s, axis=-1)
        a = p[:, :, 0] - lam * p[:, :, 1]
        return jnp.einsum('bhqk,bhkd->bhqd', a, vf)

    o = lax.map(body, (qb, jnp.arange(nq)))
    return o.transpose(1, 0, 3, 2, 4).reshape(Bsz, S, H, 2 * dq)


def setup_inputs(seed: int = 0) -> dict:
    key = jax.random.key(seed)
    ks = jax.random.split(key, 16)
    f32 = jnp.float32
    nrm = lambda k, shape, s: jax.random.normal(k, shape, f32) * s
    gain = lambda k, shape: 1.0 + 0.05 * jax.random.normal(k, shape, f32)
    return {
        "x": jax.random.normal(ks[0], (BATCH, SEQ, D_MODEL), f32),
        "rel_bias": nrm(ks[1], (NUM_BUCKETS, N_HEADS_DIL + N_HEADS_DIFF), 0.3),
        "g_pre_mix": gain(ks[2], (DEPTH, D_MODEL)),
        "w_in": nrm(ks[3], (DEPTH, D_MODEL, IN_PROJ_WIDTH), D_MODEL ** -0.5),
        "lambda_q1": nrm(ks[4], (DEPTH, DIFF_QK_DIM), 0.1),
        "lambda_k1": nrm(ks[5], (DEPTH, DIFF_QK_DIM), 0.1),
        "lambda_q2": nrm(ks[6], (DEPTH, DIFF_QK_DIM), 0.1),
        "lambda_k2": nrm(ks[7], (DEPTH, DIFF_QK_DIM), 0.1),
        "g_subln": gain(ks[8], (DEPTH, 2 * DIFF_QK_DIM)),
        "w_out": nrm(ks[9], (DEPTH, MIX_WIDTH, D_MODEL), MIX_WIDTH ** -0.5),
        "g_post_mix": gain(ks[10], (DEPTH, D_MODEL)),
        "g_pre_mlp": gain(ks[11], (DEPTH, D_MODEL)),
        "w_up": nrm(ks[12], (DEPTH, D_MODEL, D_FF), D_MODEL ** -0.5),
        "w_down": nrm(ks[13], (DEPTH, D_FF, D_MODEL), D_FF ** -0.5),
        "g_post_mlp": gain(ks[14], (DEPTH, D_MODEL)),
    }


def reference(x, rel_bias, g_pre_mix, w_in, lambda_q1, lambda_k1, lambda_q2, lambda_k2,
              g_subln, w_out, g_post_mix, g_pre_mlp, w_up, w_down, g_post_mlp):
    Bsz, S, _ = x.shape
    table_dil = rel_bias[:, :N_HEADS_DIL]
    table_diff = rel_bias[:, N_HEADS_DIL:]
    splits = np.cumsum([DIL_WIDTH, DIL_WIDTH, DIL_WIDTH, DIFF_WIDTH, DIFF_WIDTH]).tolist()
    for l in range(DEPTH):
        h = rmsnorm(x, g_pre_mix[l])
        proj = h @ w_in[l]
        qa, ka, va, qd, kd, vd = jnp.split(proj, splits, axis=-1)
        heads_a = lambda z: z.reshape(Bsz, S, N_HEADS_DIL, HEAD_DIM).transpose(0, 2, 1, 3)
        qa = heads_a(qa) * (HEAD_DIM ** -0.5)
        ka, va = heads_a(ka), heads_a(va)
        out_a = dilated_attention(qa, ka, va, table_dil)
        out_a = out_a.transpose(0, 2, 1, 3).reshape(Bsz, S, DIL_WIDTH)

        heads_d = lambda z: z.reshape(Bsz, S, N_HEADS_DIFF, 2, DIFF_QK_DIM).transpose(0, 2, 3, 1, 4)
        qd = heads_d(qd) * (DIFF_QK_DIM ** -0.5)
        kd = heads_d(kd)
        vd = vd.reshape(Bsz, S, N_HEADS_DIFF, 2 * DIFF_QK_DIM).transpose(0, 2, 1, 3)
        lambda_init = 0.8 - 0.6 * math.exp(-0.3 * l)
        lam = (jnp.exp(jnp.sum(lambda_q1[l].astype(jnp.float32) * lambda_k1[l].astype(jnp.float32)))
               - jnp.exp(jnp.sum(lambda_q2[l].astype(jnp.float32) * lambda_k2[l].astype(jnp.float32)))
               + lambda_init)
        out_d = diff_attention(qd, kd, vd, table_diff, lam)
        out_d = rmsnorm(out_d, g_subln[l], SUBLN_EPS) * (1.0 - lambda_init)
        out_d = out_d.reshape(Bsz, S, DIFF_WIDTH)

        mixed = jnp.concatenate([out_a, out_d.astype(out_a.dtype)], axis=-1).astype(x.dtype)
        x = x + rmsnorm(mixed @ w_out[l], g_post_mix[l])

        h = rmsnorm(x, g_pre_mlp[l])
        u = jnp.square(jax.nn.relu(h @ w_up[l]))
        x = x + rmsnorm(u @ w_down[l], g_post_mlp[l])
    return x
```

```python
import functools
import math

import numpy as np
import jax
import jax.numpy as jnp
from jax import lax
from jax.experimental import pallas as pl
from jax.experimental.pallas import tpu as pltpu

D_MODEL = 4096
DEPTH = 2
HEAD_DIM = 128
N_HEADS_DIL = 16
N_HEADS_DIFF = 16
DIFF_QK_DIM = 64
DIL_WIDTH = N_HEADS_DIL * HEAD_DIM
DIFF_WIDTH = N_HEADS_DIFF * HEAD_DIM
DILATIONS = (16, 4, 1)
DIL_RADIUS = 64
NUM_BUCKETS = 32
MAX_DISTANCE = 1024
RMS_EPS = 1e-6
SUBLN_EPS = 1e-5
NEG_INF = -1e30

LANES = 128
DIL_QBLK = 128
DIL_KWIN = 256
DIFF_QBLK = 128
VMEM_LIMIT = 56 * 1024 * 1024

F32 = jnp.float32
BF16 = jnp.bfloat16


def _cparams(sem):
    return pltpu.CompilerParams(dimension_semantics=sem, vmem_limit_bytes=VMEM_LIMIT)


def _rmsnorm_kernel(x_ref, g_ref, o_ref):
    x = x_ref[...]
    ms = jnp.mean(x * x, axis=-1, keepdims=True)
    o_ref[...] = (x * lax.rsqrt(ms + RMS_EPS) * g_ref[...]).astype(o_ref.dtype)


def _rmsnorm(x, g, tm=256):
    m, d = x.shape
    return pl.pallas_call(
        _rmsnorm_kernel,
        grid=(m // tm,),
        in_specs=[pl.BlockSpec((tm, d), lambda i: (i, 0)),
                  pl.BlockSpec((1, d), lambda i: (0, 0))],
        out_specs=pl.BlockSpec((tm, d), lambda i: (i, 0)),
        out_shape=jax.ShapeDtypeStruct((m, d), BF16),
        compiler_params=_cparams(("parallel",)),
        name="rmsnorm",
    )(x, g.reshape(1, d))


def _mm_epilogue(r, s_ref, act, dtype):
    if s_ref is not None:
        r = r * s_ref[...]
    if act == "relu2":
        r = jnp.square(jnp.maximum(r, 0.0))
    return r.astype(dtype)


def _mm_kernel(*refs, nk, act, scaled):
    if scaled:
        a_ref, b_ref, s_ref, o_ref = refs[:4]
        rest = refs[4:]
    else:
        a_ref, b_ref, o_ref = refs[:3]
        s_ref = None
        rest = refs[3:]
    r = jnp.dot(a_ref[...], b_ref[...], preferred_element_type=F32)
    if nk == 1:
        o_ref[...] = _mm_epilogue(r, s_ref, act, o_ref.dtype)
        return
    acc_ref, = rest
    k = pl.program_id(2)

    @pl.when(k == 0)
    def _():
        acc_ref[...] = r

    @pl.when(jnp.logical_and(k > 0, k < nk - 1))
    def _():
        acc_ref[...] += r

    @pl.when(k == nk - 1)
    def _():
        o_ref[...] = _mm_epilogue(acc_ref[...] + r, s_ref, act, o_ref.dtype)


def _matmul(a, b, out_dtype, colscale=None, act=None, tm=1024, tn=1024, tk=None):
    m, kdim = a.shape
    _, n = b.shape
    if tk is None:
        tk = kdim if kdim <= 4096 else 2048
    nk = kdim // tk
    scaled = colscale is not None
    in_specs = [pl.BlockSpec((tm, tk), lambda i, j, k: (i, k)),
                pl.BlockSpec((tk, tn), lambda i, j, k: (k, j))]
    args = [a, b]
    if scaled:
        in_specs.append(pl.BlockSpec((1, tn), lambda i, j, k: (0, j)))
        args.append(colscale.reshape(1, n).astype(F32))
    scratch = [pltpu.VMEM((tm, tn), F32)] if nk > 1 else []
    return pl.pallas_call(
        functools.partial(_mm_kernel, nk=nk, act=act, scaled=scaled),
        grid=(m // tm, n // tn, nk),
        in_specs=in_specs,
        out_specs=pl.BlockSpec((tm, tn), lambda i, j, k: (i, j)),
        out_shape=jax.ShapeDtypeStruct((m, n), out_dtype),
        scratch_shapes=scratch,
        compiler_params=_cparams(("parallel", "parallel", "arbitrary")),
        name="matmul",
    )(*args)


def _norm_res_kernel(x_ref, y_ref, g_ref, *rest, with_next):
    y = y_ref[...]
    ms = jnp.mean(y * y, axis=-1, keepdims=True)
    xn = x_ref[...] + y * lax.rsqrt(ms + RMS_EPS) * g_ref[...]
    if with_next:
        g2_ref, o_ref, h_ref = rest
        o_ref[...] = xn
        ms2 = jnp.mean(xn * xn, axis=-1, keepdims=True)
        h_ref[...] = (xn * lax.rsqrt(ms2 + RMS_EPS) * g2_ref[...]).astype(h_ref.dtype)
    else:
        o_ref, = rest
        o_ref[...] = xn


def _norm_res(x, y, g, g_next=None, tm=128):
    m, d = x.shape
    with_next = g_next is not None
    row = pl.BlockSpec((tm, d), lambda i: (i, 0))
    vec = pl.BlockSpec((1, d), lambda i: (0, 0))
    in_specs = [row, row, vec]
    args = [x, y, g.reshape(1, d)]
    out_shape = jax.ShapeDtypeStruct((m, d), F32)
    out_specs = row
    if with_next:
        in_specs.append(vec)
        args.append(g_next.reshape(1, d))
        out_shape = (out_shape, jax.ShapeDtypeStruct((m, d), BF16))
        out_specs = (row, row)
    return pl.pallas_call(
        functools.partial(_norm_res_kernel, with_next=with_next),
        grid=(m // tm,),
        in_specs=in_specs,
        out_specs=out_specs,
        out_shape=out_shape,
        compiler_params=_cparams(("parallel",)),
        name="norm_res",
    )(*args)


def _t5_bucket_np(rel):
    half = NUM_BUCKETS // 2
    max_exact = half // 2
    ret = np.where(rel > 0, half, 0)
    n = np.abs(rel)
    nf = np.maximum(n, 1).astype(np.float32)
    large = max_exact + (np.log(nf / np.float32(max_exact)) / np.float32(math.log(MAX_DISTANCE / max_exact))
                         * np.float32(half - max_exact)).astype(np.int32)
    large = np.minimum(large, half - 1)
    return (ret + np.where(n < max_exact, n, large)).astype(np.int32)


def _bias_kernel(tab_ref, idx_ref, o_ref, *, col0):
    h = pl.program_id(0)
    idx = idx_ref[...]
    acc = jnp.full(idx.shape, NEG_INF, F32)
    for b in range(NUM_BUCKETS):
        acc = jnp.where(idx == b, tab_ref[b, col0 + h], acc)
    o_ref[0] = acc


def _build_bias(rel_bias, idx, n_heads, col0, cb):
    r, c = idx.shape
    return pl.pallas_call(
        functools.partial(_bias_kernel, col0=col0),
        grid=(n_heads, c // cb),
        in_specs=[pl.BlockSpec(memory_space=pltpu.SMEM),
                  pl.BlockSpec((r, cb), lambda h, j: (0, j))],
        out_specs=pl.BlockSpec((1, r, cb), lambda h, j: (h, 0, j)),
        out_shape=jax.ShapeDtypeStruct((n_heads, r, c), F32),
        compiler_params=_cparams(("parallel", "parallel")),
        name="bias_tiles",
    )(rel_bias, jnp.asarray(idx))


def _dil_bias_idx():
    row = np.arange(DIL_QBLK)[:, None]
    col = np.arange(DIL_KWIN)[None, :]
    tiles = []
    for d in DILATIONS:
        for off in (0, -DIL_RADIUS, -2 * DIL_RADIUS):
            rel = off + col - row
            idx = _t5_bucket_np(rel * d)
            tiles.append(np.where(np.abs(rel) <= DIL_RADIUS, idx, -1))
    return np.concatenate(tiles, axis=0).astype(np.int32)


def _diff_bias_idx(seq):
    row = np.arange(DIFF_QBLK)[:, None]
    col = np.arange(2 * seq)[None, :]
    return _t5_bucket_np(col - seq - row)


def _dil_kernel(q_ref, k_ref, v_ref, bias_ref, o_ref, m_scr, l_scr, n_scr, *, seq):
    n_it = seq // DIL_QBLK
    for bi, d in enumerate(DILATIONS):
        sub_len = seq // d
        nblk = sub_len // DIL_QBLK
        log_d = d.bit_length() - 1
        first = bi == 0
        last = bi == len(DILATIONS) - 1

        def rows(start, size, d=d):
            if d == 1:
                return pl.ds(pl.multiple_of(start, DIL_QBLK // 2), size)
            return pl.ds(start, size, stride=d)

        def body(it, carry, d=d, nblk=nblk, log_d=log_d, first=first, last=last, bi=bi,
                 sub_len=sub_len, rows=rows):
            r = it & (d - 1)
            j = it >> log_d
            q0 = j * DIL_QBLK
            ws = jnp.clip(q0 - DIL_RADIUS, 0, sub_len - DIL_KWIN)
            variant = jnp.where(j == 0, 0, jnp.where(j == nblk - 1, 2, 1))
            qsl = rows(r + d * q0, DIL_QBLK)
            ksl = rows(r + d * ws, DIL_KWIN)
            qb = q_ref[qsl, :].astype(BF16)
            kb = k_ref[ksl, :].astype(BF16)
            vb = v_ref[ksl, :].astype(BF16)
            s = lax.dot_general(qb, kb, (((1,), (1,)), ((), ())), preferred_element_type=F32)
            s = s + bias_ref[0, bi * 3 + variant]
            m = jnp.max(s, axis=1, keepdims=True)
            p = jnp.exp(s - m)
            l = jnp.sum(p, axis=1, keepdims=True)
            n = jnp.dot(p.astype(BF16), vb, preferred_element_type=F32)
            shape = (DIL_QBLK, LANES)
            m = jnp.broadcast_to(m, shape)
            l = jnp.broadcast_to(l, shape)
            if not first:
                m_old = m_scr[qsl, :]
                m_new = jnp.maximum(m_old, m)
                a = jnp.exp(m_old - m_new)
                b = jnp.exp(m - m_new)
                l = a * l_scr[qsl, :] + b * l
                n = a * n_scr[qsl, :] + b * n
                m = m_new
            if last:
                o_ref[qsl, :] = (n / l).astype(o_ref.dtype)
            else:
                m_scr[qsl, :] = m
                l_scr[qsl, :] = l
                n_scr[qsl, :] = n
            return carry

        lax.fori_loop(0, n_it, body, 0)


def _dil_attention(proj, bias_tiles, seq):
    bsz = proj.shape[0]
    nh = N_HEADS_DIL

    def col(off):
        return pl.BlockSpec((None, seq, HEAD_DIM), lambda b, h: (b, 0, off + h))

    state = pltpu.VMEM((seq, LANES), F32)
    return pl.pallas_call(
        functools.partial(_dil_kernel, seq=seq),
        grid=(bsz, nh),
        in_specs=[col(0), col(nh), col(2 * nh),
                  pl.BlockSpec((1, 3 * len(DILATIONS), DIL_QBLK, DIL_KWIN), lambda b, h: (h, 0, 0, 0))],
        out_specs=pl.BlockSpec((None, seq, HEAD_DIM), lambda b, h: (b, 0, h)),
        out_shape=jax.ShapeDtypeStruct((bsz, seq, DIL_WIDTH), BF16),
        scratch_shapes=[state, state, state],
        compiler_params=_cparams(("parallel", "parallel")),
        name="dilated_attention",
    )(proj, proj, proj, bias_tiles)


def _diff_kernel(lam_ref, q_ref, k_ref, v_ref, t_ref, g_ref, o_ref, *, seq, lambda_init):
    qi = pl.program_id(2)
    lq1 = lam_ref[0:1, :]
    lk1 = lam_ref[1:2, :]
    lq2 = lam_ref[2:3, :]
    lk2 = lam_ref[3:4, :]
    lam = (jnp.exp(jnp.sum(lq1 * lk1, axis=1, keepdims=True))
           - jnp.exp(jnp.sum(lq2 * lk2, axis=1, keepdims=True)) + lambda_init)

    q = q_ref[...]
    lane = lax.broadcasted_iota(jnp.int32, q.shape, 1)
    zero = jnp.zeros_like(q)
    start = pl.multiple_of(seq - qi * DIFF_QBLK, LANES)
    bias = t_ref[0, :, pl.ds(start, seq)]
    k = k_ref[...]
    v = v_ref[...]

    def one_map(qm):
        s = lax.dot_general(qm, k, (((1,), (1,)), ((), ())), preferred_element_type=F32)
        s = s + bias
        m = jnp.max(s, axis=1, keepdims=True)
        p = jnp.exp(s - m)
        l = jnp.sum(p, axis=1, keepdims=True)
        return jnp.dot(p.astype(BF16), v, preferred_element_type=F32) / l

    o = one_map(jnp.where(lane < DIFF_QK_DIM, q, zero)) - lam * one_map(jnp.where(lane >= DIFF_QK_DIM, q, zero))
    ms = jnp.mean(o * o, axis=-1, keepdims=True)
    y = o * lax.rsqrt(ms + SUBLN_EPS) * g_ref[...]
    o_ref[...] = (y * (1.0 - lambda_init)).astype(o_ref.dtype)


def _diff_attention(proj, t_bias, lam_vecs, g_subln, seq, lambda_init):
    bsz = proj.shape[0]
    nh = N_HEADS_DIFF
    nq = seq // DIFF_QBLK
    return pl.pallas_call(
        functools.partial(_diff_kernel, seq=seq, lambda_init=lambda_init),
        grid=(nh, bsz, nq),
        in_specs=[pl.BlockSpec((4, DIFF_QK_DIM), lambda h, b, i: (0, 0)),
                  pl.BlockSpec((None, DIFF_QBLK, HEAD_DIM), lambda h, b, i: (b, i, h)),
                  pl.BlockSpec((None, seq, HEAD_DIM), lambda h, b, i: (b, 0, nh + h)),
                  pl.BlockSpec((None, seq, HEAD_DIM), lambda h, b, i: (b, 0, 2 * nh + h)),
                  pl.BlockSpec((1, DIFF_QBLK, 2 * seq), lambda h, b, i: (h, 0, 0)),
                  pl.BlockSpec((1, HEAD_DIM), lambda h, b, i: (0, 0))],
        out_specs=pl.BlockSpec((None, DIFF_QBLK, HEAD_DIM), lambda h, b, i: (b, i, h)),
        out_shape=jax.ShapeDtypeStruct((bsz, seq, DIFF_WIDTH), BF16),
        compiler_params=_cparams(("parallel", "parallel", "parallel")),
        name="diff_attention",
    )(lam_vecs, proj, proj, proj, t_bias, g_subln.reshape(1, HEAD_DIM))


def kernel(x, rel_bias, g_pre_mix, w_in, lambda_q1, lambda_k1, lambda_q2, lambda_k2, g_subln, w_out,
           g_post_mix, g_pre_mlp, w_up, w_down, g_post_mlp):
    bsz, seq, d = x.shape
    m = bsz * seq
    xf = x.reshape(m, d)

    dil_tiles = _build_bias(rel_bias, _dil_bias_idx(), N_HEADS_DIL, 0, DIL_KWIN)
    dil_tiles = dil_tiles.reshape(N_HEADS_DIL, 3 * len(DILATIONS), DIL_QBLK, DIL_KWIN)
    diff_t = _build_bias(rel_bias, _diff_bias_idx(seq), N_HEADS_DIFF, N_HEADS_DIL, 1024)

    ones = jnp.ones((DIL_WIDTH,), F32)
    scale_dil = jnp.concatenate([ones * (HEAD_DIM ** -0.5), ones, ones])
    scale_diff = jnp.concatenate([ones * (DIFF_QK_DIM ** -0.5), ones, ones])

    h = _rmsnorm(xf, g_pre_mix[0])
    for l in range(DEPTH):
        lambda_init = 0.8 - 0.6 * math.exp(-0.3 * l)
        w_dil = w_in[l, :, :3 * DIL_WIDTH].astype(BF16)
        w_diff = w_in[l, :, 3 * DIL_WIDTH:].astype(BF16)
        proj_a = _matmul(h, w_dil, F32, colscale=scale_dil)
        proj_d = _matmul(h, w_diff, BF16, colscale=scale_diff)
        out_a = _dil_attention(proj_a.reshape(bsz, seq, 3 * DIL_WIDTH), dil_tiles, seq)
        lam_vecs = jnp.stack([lambda_q1[l], lambda_k1[l], lambda_q2[l], lambda_k2[l]]).astype(F32)
        out_d = _diff_attention(proj_d.reshape(bsz, seq, 3 * DIFF_WIDTH), diff_t, lam_vecs, g_subln[l],
                                seq, lambda_init)
        mixed = jnp.concatenate([out_a, out_d], axis=-1).reshape(m, DIL_WIDTH + DIFF_WIDTH)
        y = _matmul(mixed, w_out[l].astype(BF16), F32)
        xf, h = _norm_res(xf, y, g_post_mix[l], g_pre_mlp[l])
        u = _matmul(h, w_up[l].astype(BF16), BF16, act="relu2")
        y = _matmul(u, w_down[l].astype(BF16), F32)
        if l + 1 < DEPTH:
            xf, h = _norm_res(xf, y, g_post_mlp[l], g_pre_mix[l + 1])
        else:
            xf = _norm_res(xf, y, g_post_mlp[l])
    return xf.reshape(bsz, seq, d)
```

```python
import functools
import math

import numpy as np
import jax
import jax.numpy as jnp
from jax import lax
from jax.experimental import pallas as pl
from jax.experimental.pallas import tpu as pltpu

D_MODEL = 4096
DEPTH = 2
HEAD_DIM = 128
N_HEADS_DIL = 16
N_HEADS_DIFF = 16
DIFF_QK_DIM = 64
DIL_WIDTH = N_HEADS_DIL * HEAD_DIM
DIFF_WIDTH = N_HEADS_DIFF * HEAD_DIM
DILATIONS = (16, 4, 1)
DIL_RADIUS = 64
NUM_BUCKETS = 32
MAX_DISTANCE = 1024
RMS_EPS = 1e-6
SUBLN_EPS = 1e-5
NEG_INF = -1e30

LANES = 128
DIL_QBLK = 128
DIL_KWIN = 256
DIL_UNROLL = 4
DIFF_QBLK = 512
DIFF_FAR = 768
DIFF_BAND = 2 * DIFF_FAR + DIFF_QBLK
DIFF_CHUNK = 512
LOG2E = 1.4426950408889634
VMEM_LIMIT = 56 * 1024 * 1024

F32 = jnp.float32
BF16 = jnp.bfloat16


def _cparams(sem, flags=None):
    return pltpu.CompilerParams(dimension_semantics=sem, vmem_limit_bytes=VMEM_LIMIT, flags=flags)


def _rmsnorm_kernel(x_ref, g_ref, o_ref):
    x = x_ref[...]
    ms = jnp.mean(x * x, axis=-1, keepdims=True)
    o_ref[...] = (x * lax.rsqrt(ms + RMS_EPS) * g_ref[...]).astype(o_ref.dtype)


def _rmsnorm(x, g, tm=256):
    m, d = x.shape
    return pl.pallas_call(
        _rmsnorm_kernel,
        grid=(m // tm,),
        in_specs=[pl.BlockSpec((tm, d), lambda i: (i, 0)),
                  pl.BlockSpec((1, d), lambda i: (0, 0))],
        out_specs=pl.BlockSpec((tm, d), lambda i: (i, 0)),
        out_shape=jax.ShapeDtypeStruct((m, d), BF16),
        compiler_params=_cparams(("parallel",)),
        name="rmsnorm",
    )(x, g.reshape(1, d))


def _mm_epilogue(r, s_ref, act, dtype):
    if s_ref is not None:
        r = r * s_ref[...]
    if act == "relu2":
        r = jnp.square(jnp.maximum(r, 0.0))
    return r.astype(dtype)


def _mm_kernel(*refs, nk, act, scaled):
    if scaled:
        a_ref, b_ref, s_ref, o_ref = refs[:4]
        rest = refs[4:]
    else:
        a_ref, b_ref, o_ref = refs[:3]
        s_ref = None
        rest = refs[3:]
    r = jnp.dot(a_ref[...], b_ref[...], preferred_element_type=F32)
    if nk == 1:
        o_ref[...] = _mm_epilogue(r, s_ref, act, o_ref.dtype)
        return
    acc_ref, = rest
    k = pl.program_id(2)

    @pl.when(k == 0)
    def _():
        acc_ref[...] = r

    @pl.when(jnp.logical_and(k > 0, k < nk - 1))
    def _():
        acc_ref[...] += r

    @pl.when(k == nk - 1)
    def _():
        o_ref[...] = _mm_epilogue(acc_ref[...] + r, s_ref, act, o_ref.dtype)


def _cast_kernel(x_ref, o_ref):
    o_ref[...] = x_ref[...].astype(o_ref.dtype)


def _to_bf16(w, layer, tr=512, tc=4096):
    _, r, c = w.shape
    tc = min(tc, c)
    return pl.pallas_call(
        _cast_kernel,
        grid=(r // tr, c // tc),
        in_specs=[pl.BlockSpec((None, tr, tc), lambda i, j: (layer, i, j))],
        out_specs=pl.BlockSpec((tr, tc), lambda i, j: (i, j)),
        out_shape=jax.ShapeDtypeStruct((r, c), BF16),
        compiler_params=_cparams(("parallel", "parallel")),
        name="weight_to_bf16",
    )(w)


def _matmul(a, b, out_dtype, colscale=None, act=None, b_cols=None, tm=1024, tn=1024, tk=None):
    m, kdim = a.shape
    c0, n = b_cols if b_cols is not None else (0, b.shape[1])
    jb = c0 // tn
    assert c0 % tn == 0 and n % tn == 0 and m % tm == 0
    if tk is None:
        tk = kdim if kdim <= 4096 else 2048
    nk = kdim // tk
    scaled = colscale is not None
    in_specs = [pl.BlockSpec((tm, tk), lambda i, j, k: (i, k)),
                pl.BlockSpec((tk, tn), lambda i, j, k: (k, j + jb))]
    args = [a, b]
    if scaled:
        in_specs.append(pl.BlockSpec((1, tn), lambda i, j, k: (0, j)))
        args.append(colscale.reshape(1, n).astype(F32))
    scratch = [pltpu.VMEM((tm, tn), F32)] if nk > 1 else []
    return pl.pallas_call(
        functools.partial(_mm_kernel, nk=nk, act=act, scaled=scaled),
        grid=(m // tm, n // tn, nk),
        in_specs=in_specs,
        out_specs=pl.BlockSpec((tm, tn), lambda i, j, k: (i, j)),
        out_shape=jax.ShapeDtypeStruct((m, n), out_dtype),
        scratch_shapes=scratch,
        compiler_params=_cparams(("parallel", "parallel", "arbitrary")),
        name="matmul",
    )(*args)


def _norm_res_kernel(x_ref, y_ref, g_ref, *rest, with_next):
    y = y_ref[...]
    ms = jnp.mean(y * y, axis=-1, keepdims=True)
    xn = x_ref[...] + y * lax.rsqrt(ms + RMS_EPS) * g_ref[...]
    if with_next:
        g2_ref, o_ref, h_ref = rest
        o_ref[...] = xn
        ms2 = jnp.mean(xn * xn, axis=-1, keepdims=True)
        h_ref[...] = (xn * lax.rsqrt(ms2 + RMS_EPS) * g2_ref[...]).astype(h_ref.dtype)
    else:
        o_ref, = rest
        o_ref[...] = xn


def _norm_res(x, y, g, g_next=None, tm=128):
    m, d = x.shape
    with_next = g_next is not None
    row = pl.BlockSpec((tm, d), lambda i: (i, 0))
    vec = pl.BlockSpec((1, d), lambda i: (0, 0))
    in_specs = [row, row, vec]
    args = [x, y, g.reshape(1, d)]
    out_shape = jax.ShapeDtypeStruct((m, d), F32)
    out_specs = row
    if with_next:
        in_specs.append(vec)
        args.append(g_next.reshape(1, d))
        out_shape = (out_shape, jax.ShapeDtypeStruct((m, d), BF16))
        out_specs = (row, row)
    return pl.pallas_call(
        functools.partial(_norm_res_kernel, with_next=with_next),
        grid=(m // tm,),
        in_specs=in_specs,
        out_specs=out_specs,
        out_shape=out_shape,
        compiler_params=_cparams(("parallel",)),
        name="norm_res",
    )(*args)


def _t5_bucket_np(rel):
    half = NUM_BUCKETS // 2
    max_exact = half // 2
    ret = np.where(rel > 0, half, 0)
    n = np.abs(rel)
    nf = np.maximum(n, 1).astype(np.float32)
    large = max_exact + (np.log(nf / np.float32(max_exact)) / np.float32(math.log(MAX_DISTANCE / max_exact))
                         * np.float32(half - max_exact)).astype(np.int32)
    large = np.minimum(large, half - 1)
    return (ret + np.where(n < max_exact, n, large)).astype(np.int32)


def _bias_kernel(tab_ref, idx_ref, o_ref, *, col0, scale):
    h = pl.program_id(0)
    idx = idx_ref[...]
    acc = jnp.full(idx.shape, NEG_INF, F32)
    for b in range(NUM_BUCKETS):
        acc = jnp.where(idx == b, tab_ref[b, col0 + h] * scale, acc)
    o_ref[0] = acc


def _build_bias(rel_bias, idx, n_heads, col0, rb, scale=1.0):
    r, c = idx.shape
    return pl.pallas_call(
        functools.partial(_bias_kernel, col0=col0, scale=scale),
        grid=(n_heads, r // rb),
        in_specs=[pl.BlockSpec(memory_space=pltpu.SMEM),
                  pl.BlockSpec((rb, c), lambda h, j: (j, 0))],
        out_specs=pl.BlockSpec((1, rb, c), lambda h, j: (h, j, 0)),
        out_shape=jax.ShapeDtypeStruct((n_heads, r, c), F32),
        compiler_params=_cparams(("parallel", "parallel")),
        name="bias_tiles",
    )(rel_bias, jnp.asarray(idx))


def _dil_bias_idx():
    row = np.arange(DIL_QBLK)[:, None]
    col = np.arange(DIL_KWIN)[None, :]
    tiles = []
    for d in DILATIONS:
        for off in (0, -DIL_RADIUS, -2 * DIL_RADIUS):
            rel = off + col - row
            idx = _t5_bucket_np(rel * d)
            tiles.append(np.where(np.abs(rel) <= DIL_RADIUS, idx, -1))
    return np.concatenate(tiles, axis=0).astype(np.int32)


def _diff_bias_idx(seq):
    far = np.arange(DIFF_FAR, seq)
    assert np.all(_t5_bucket_np(-far) == NUM_BUCKETS // 2 - 1) and np.all(_t5_bucket_np(far) == NUM_BUCKETS - 1)
    row = np.arange(DIFF_BAND + 2 * DIFF_FAR)[:, None]
    col = np.arange(DIFF_QBLK)[None, :]
    return _t5_bucket_np(row - 2 * DIFF_FAR - col)


def _diff_key_steps(seq):
    key = np.arange(seq)[:, None]
    lane = np.arange(LANES)[None, :]
    return np.where((lane < 64) & (key >= LANES * (lane & 31)), 1.0, 0.0).astype(np.float32)


def _dil_kernel(q_ref, k_ref, v_ref, bias_ref, o_ref, m_scr, l_scr, n_scr, *, seq):
    n_it = seq // DIL_QBLK
    for bi, d in enumerate(DILATIONS):
        sub_len = seq // d
        nblk = sub_len // DIL_QBLK
        log_d = d.bit_length() - 1
        first = bi == 0
        last = bi == len(DILATIONS) - 1

        def rows(start, size, d=d):
            if d == 1:
                return pl.ds(pl.multiple_of(start, DIL_QBLK // 2), size)
            return pl.ds(start, size, stride=d)

        def body(grp, carry, d=d, nblk=nblk, log_d=log_d, first=first, last=last, bi=bi,
                 sub_len=sub_len, rows=rows):
            qsls, scores, vbs = [], [], []
            for u in range(DIL_UNROLL):
                it = grp * DIL_UNROLL + u
                r = it & (d - 1)
                j = it >> log_d
                q0 = j * DIL_QBLK
                ws = jnp.clip(q0 - DIL_RADIUS, 0, sub_len - DIL_KWIN)
                variant = jnp.where(j == 0, 0, jnp.where(j == nblk - 1, 2, 1))
                qsl = rows(r + d * q0, DIL_QBLK)
                ksl = rows(r + d * ws, DIL_KWIN)
                qb = q_ref[qsl, :].astype(BF16)
                kb = k_ref[ksl, :].astype(BF16)
                s = lax.dot_general(qb, kb, (((1,), (1,)), ((), ())), preferred_element_type=F32)
                qsls.append(qsl)
                scores.append(s + bias_ref[0, bi * 3 + variant])
                vbs.append(v_ref[ksl, :].astype(BF16))
            stats = []
            for s in scores:
                m = jnp.max(s, axis=1, keepdims=True)
                p = jnp.exp(s - m)
                stats.append((m, jnp.sum(p, axis=1, keepdims=True), p.astype(BF16)))
            nums = [jnp.dot(p, vb, preferred_element_type=F32) for (_, _, p), vb in zip(stats, vbs)]
            shape = (DIL_QBLK, LANES)
            for qsl, (m, l, _), n in zip(qsls, stats, nums):
                m = jnp.broadcast_to(m, shape)
                l = jnp.broadcast_to(l, shape)
                if not first:
                    m_old = m_scr[qsl, :]
                    m_new = jnp.maximum(m_old, m)
                    a = jnp.exp(m_old - m_new)
                    b = jnp.exp(m - m_new)
                    l = a * l_scr[qsl, :] + b * l
                    n = a * n_scr[qsl, :] + b * n
                    m = m_new
                if last:
                    o_ref[qsl, :] = (n / l).astype(o_ref.dtype)
                else:
                    m_scr[qsl, :] = m
                    l_scr[qsl, :] = l
                    n_scr[qsl, :] = n
            return carry

        lax.fori_loop(0, n_it // DIL_UNROLL, body, 0)


def _dil_attention(proj, bias_tiles, seq):
    bsz = proj.shape[0]
    nh = N_HEADS_DIL

    def col(off):
        return pl.BlockSpec((None, seq, HEAD_DIM), lambda b, h: (b, 0, off + h))

    state = pltpu.VMEM((seq, LANES), F32)
    return pl.pallas_call(
        functools.partial(_dil_kernel, seq=seq),
        grid=(bsz, nh),
        in_specs=[col(0), col(nh), col(2 * nh),
                  pl.BlockSpec((1, 3 * len(DILATIONS), DIL_QBLK, DIL_KWIN), lambda b, h: (h, 0, 0, 0))],
        out_specs=pl.BlockSpec((None, seq, HEAD_DIM), lambda b, h: (b, 0, h)),
        out_shape=jax.ShapeDtypeStruct((bsz, seq, DIL_WIDTH), BF16),
        scratch_shapes=[state, state, state],
        compiler_params=_cparams(("parallel", "parallel")),
        name="dilated_attention",
    )(proj, proj, proj, bias_tiles)


def _diff_kernel(tab_ref, lam_ref, q_ref, k_ref, ksteps_ref, vt_ref, t_ref, g_ref, o_ref, *, seq, lambda_init):
    h = pl.program_id(0)
    q0 = pl.program_id(2) * DIFF_QBLK
    ks = jnp.clip(q0 - DIFF_FAR, 0, seq - DIFF_BAND)
    ts = ks - q0 + 2 * DIFF_FAR
    step_lo = lax.shift_right_logical(ks, 7)
    step_hi = step_lo + DIFF_BAND // LANES
    n_piece = seq // LANES

    lam = (jnp.exp(jnp.sum(lam_ref[0:1, :] * lam_ref[1:2, :], axis=1, keepdims=True))
           - jnp.exp(jnp.sum(lam_ref[2:3, :] * lam_ref[3:4, :], axis=1, keepdims=True)) + lambda_init)

    c_left = tab_ref[NUM_BUCKETS // 2 - 1, N_HEADS_DIL + h] * LOG2E
    c_right = tab_ref[NUM_BUCKETS - 1, N_HEADS_DIL + h] * LOG2E
    lane = lax.broadcasted_iota(jnp.int32, (1, LANES), 1)
    lj = lane & 31
    c = (jnp.where(lj == 0, c_left, 0.0) + jnp.where(lj == step_lo, -c_left, 0.0)
         + jnp.where(lj == step_hi, c_right, 0.0))
    c = jnp.where(lane < 64, c, 0.0)
    c_hi = c.astype(BF16).astype(F32)
    q_extra = jnp.where(lane < 32, c_hi, c - c_hi).astype(BF16)
    q_extra = jnp.broadcast_to(q_extra, (DIFF_QBLK, LANES))

    q = q_ref[...]
    qlane = lax.broadcasted_iota(jnp.int32, q.shape, 1)
    zero = jnp.zeros_like(q)
    q_both = jnp.concatenate(
        [jnp.concatenate([jnp.where(qlane < DIFF_QK_DIM, q, zero), q_extra], axis=1),
         jnp.concatenate([jnp.where(qlane >= DIFF_QK_DIM, q, zero), q_extra], axis=1)], axis=0)

    pieces_per_chunk = DIFF_CHUNK // LANES

    def piece_start(g):
        return pl.multiple_of(((step_lo + g) & (n_piece - 1)) * LANES, LANES)

    def scores(c0):
        k_c = jnp.concatenate(
            [jnp.concatenate([k_ref[pl.ds(piece_start(g), LANES), :], ksteps_ref[pl.ds(piece_start(g), LANES), :]],
                             axis=1) for g in range(c0, c0 + pieces_per_chunk)], axis=0)
        return lax.dot_general(k_c, q_both, (((1,), (1,)), ((), ())), preferred_element_type=F32)

    parts = []
    s_next = scores(0)
    for c0 in range(0, n_piece, pieces_per_chunk):
        s = s_next
        if c0 + pieces_per_chunk < n_piece:
            s_next = scores(c0 + pieces_per_chunk)
        vt_c = jnp.concatenate([vt_ref[:, pl.ds(piece_start(g), LANES)]
                                for g in range(c0, c0 + pieces_per_chunk)], axis=1)
        s_maps = [s[:, :DIFF_QBLK], s[:, DIFF_QBLK:]]
        if c0 * LANES < DIFF_BAND:
            tile = t_ref[0, pl.ds(pl.multiple_of(ts + c0 * LANES, LANES), DIFF_CHUNK), :]
            s_maps = [sm + tile for sm in s_maps]
        m_c = [jnp.max(sm, axis=0, keepdims=True) for sm in s_maps]
        p_c = [jnp.exp2(sm - mm) for sm, mm in zip(s_maps, m_c)]
        l_c = [jnp.sum(pm, axis=0, keepdims=True) for pm in p_c]
        p_both = jnp.concatenate([pm.astype(BF16) for pm in p_c], axis=1)
        o_c = jnp.dot(vt_c, p_both, preferred_element_type=F32)
        parts.append((m_c, l_c, o_c))

    outs = []
    for mi in range(2):
        m = functools.reduce(jnp.maximum, [pt[0][mi] for pt in parts])
        l = 0.0
        o = 0.0
        for m_c, l_c, o_c in parts:
            w = jnp.exp2(m_c[mi] - m)
            l = l + w * l_c[mi]
            o = o + w * o_c[:, mi * DIFF_QBLK:(mi + 1) * DIFF_QBLK]
        outs.append(o / l)
    o = outs[0] - lam * outs[1]
    ms = jnp.mean(o * o, axis=0, keepdims=True)
    y = o * lax.rsqrt(ms + SUBLN_EPS) * g_ref[...] * (1.0 - lambda_init)
    o_ref[...] = y.T.astype(o_ref.dtype)


def _diff_attention(rel_bias, proj, vt, t_bias, lam_vecs, g_subln, seq, lambda_init):
    bsz = proj.shape[0]
    nh = N_HEADS_DIFF
    nq = seq // DIFF_QBLK
    assert seq == 32 * LANES and DIFF_BAND % DIFF_CHUNK == 0 and seq % DIFF_CHUNK == 0
    ksteps = jnp.asarray(_diff_key_steps(seq), BF16)
    return pl.pallas_call(
        functools.partial(_diff_kernel, seq=seq, lambda_init=lambda_init),
        grid=(nh, bsz, nq),
        in_specs=[pl.BlockSpec(memory_space=pltpu.SMEM),
                  pl.BlockSpec((4, DIFF_QK_DIM), lambda h, b, i: (0, 0)),
                  pl.BlockSpec((None, DIFF_QBLK, HEAD_DIM), lambda h, b, i: (b, i, h)),
                  pl.BlockSpec((None, seq, HEAD_DIM), lambda h, b, i: (b, 0, nh + h)),
                  pl.BlockSpec((seq, LANES), lambda h, b, i: (0, 0)),
                  pl.BlockSpec((None, None, HEAD_DIM, seq), lambda h, b, i: (b, h, 0, 0)),
                  pl.BlockSpec((1, DIFF_BAND + 2 * DIFF_FAR, DIFF_QBLK), lambda h, b, i: (h, 0, 0)),
                  pl.BlockSpec((HEAD_DIM, 1), lambda h, b, i: (0, 0))],
        out_specs=pl.BlockSpec((None, DIFF_QBLK, HEAD_DIM), lambda h, b, i: (b, i, h)),
        out_shape=jax.ShapeDtypeStruct((bsz, seq, DIFF_WIDTH), BF16),
        compiler_params=_cparams(("parallel", "parallel", "parallel")),
        name="diff_attention",
    )(rel_bias, lam_vecs, proj, proj, ksteps, vt, t_bias, g_subln.reshape(HEAD_DIM, 1))


def kernel(x, rel_bias, g_pre_mix, w_in, lambda_q1, lambda_k1, lambda_q2, lambda_k2, g_subln, w_out,
           g_post_mix, g_pre_mlp, w_up, w_down, g_post_mlp):
    bsz, seq, d = x.shape
    m = bsz * seq
    xf = x.reshape(m, d)

    dil_tiles = _build_bias(rel_bias, _dil_bias_idx(), N_HEADS_DIL, 0, 3 * DIL_QBLK)
    dil_tiles = dil_tiles.reshape(N_HEADS_DIL, 3 * len(DILATIONS), DIL_QBLK, DIL_KWIN)
    diff_t = _build_bias(rel_bias, _diff_bias_idx(seq), N_HEADS_DIFF, N_HEADS_DIL,
                         (DIFF_BAND + 2 * DIFF_FAR) // 4, scale=LOG2E)

    ones = jnp.ones((DIL_WIDTH,), F32)
    scale_dil = jnp.concatenate([ones * (HEAD_DIM ** -0.5), ones, ones])
    scale_diff = jnp.concatenate([ones * (DIFF_QK_DIM ** -0.5 * LOG2E), ones, ones])

    h = _rmsnorm(xf, g_pre_mix[0])
    for l in range(DEPTH):
        lambda_init = 0.8 - 0.6 * math.exp(-0.3 * l)
        w_in_l = _to_bf16(w_in, l)
        proj_a = _matmul(h, w_in_l, F32, colscale=scale_dil, b_cols=(0, 3 * DIL_WIDTH))
        proj_d = _matmul(h, w_in_l, BF16, colscale=scale_diff, b_cols=(3 * DIL_WIDTH, 3 * DIFF_WIDTH))
        out_a = _dil_attention(proj_a.reshape(bsz, seq, 3 * DIL_WIDTH), dil_tiles, seq)
        lam_vecs = jnp.stack([lambda_q1[l], lambda_k1[l], lambda_q2[l], lambda_k2[l]]).astype(F32)
        proj_d = proj_d.reshape(bsz, seq, 3 * DIFF_WIDTH)
        vt = proj_d[:, :, 2 * DIFF_WIDTH:].reshape(bsz, seq, N_HEADS_DIFF, HEAD_DIM).transpose(0, 2, 3, 1)
        out_d = _diff_attention(rel_bias, proj_d, vt, diff_t, lam_vecs, g_subln[l], seq, lambda_init)
        mixed = jnp.concatenate([out_a, out_d], axis=-1).reshape(m, DIL_WIDTH + DIFF_WIDTH)
        y = _matmul(mixed, _to_bf16(w_out, l), F32)
        xf, h = _norm_res(xf, y, g_post_mix[l], g_pre_mlp[l])
        u = _matmul(h, _to_bf16(w_up, l), BF16, act="relu2")
        y = _matmul(u, _to_bf16(w_down, l), F32)
        if l + 1 < DEPTH:
            xf, h = _norm_res(xf, y, g_post_mlp[l], g_pre_mix[l + 1])
        else:
            xf = _norm_res(xf, y, g_post_mlp[l])
    return xf.reshape(bsz, seq, d)
```

```python
import functools
import math

import numpy as np
import jax
import jax.numpy as jnp
from jax import lax
from jax.experimental import pallas as pl
from jax.experimental.pallas import tpu as pltpu

D_MODEL = 4096
DEPTH = 2
HEAD_DIM = 128
N_HEADS_DIL = 16
N_HEADS_DIFF = 16
DIFF_QK_DIM = 64
DIL_WIDTH = N_HEADS_DIL * HEAD_DIM
DIFF_WIDTH = N_HEADS_DIFF * HEAD_DIM
DILATIONS = (16, 4, 1)
DIL_RADIUS = 64
NUM_BUCKETS = 32
MAX_DISTANCE = 1024
RMS_EPS = 1e-6
SUBLN_EPS = 1e-5
NEG_INF = -1e30

LANES = 128
DIL_QBLK = 128
DIL_KWIN = 256
DIL_UNROLL = 8
DIFF_QBLK = 512
DIFF_FAR = 768
DIFF_BAND = 2 * DIFF_FAR + DIFF_QBLK
DIFF_CHUNK = 512
DIFF_SLAB = 512
DIFF_SAFE_GAP = 100.0
LOG2E = 1.4426950408889634
VMEM_LIMIT = 56 * 1024 * 1024

F32 = jnp.float32
BF16 = jnp.bfloat16


def _cparams(sem, flags=None):
    return pltpu.CompilerParams(dimension_semantics=sem, vmem_limit_bytes=VMEM_LIMIT, flags=flags)


def _rmsnorm_kernel(x_ref, g_ref, o_ref):
    x = x_ref[...]
    ms = jnp.mean(x * x, axis=-1, keepdims=True)
    o_ref[...] = (x * lax.rsqrt(ms + RMS_EPS) * g_ref[...]).astype(o_ref.dtype)


def _rmsnorm(x, g, tm=256):
    m, d = x.shape
    return pl.pallas_call(
        _rmsnorm_kernel,
        grid=(m // tm,),
        in_specs=[pl.BlockSpec((tm, d), lambda i: (i, 0)),
                  pl.BlockSpec((1, d), lambda i: (0, 0))],
        out_specs=pl.BlockSpec((tm, d), lambda i: (i, 0)),
        out_shape=jax.ShapeDtypeStruct((m, d), BF16),
        compiler_params=_cparams(("parallel",)),
        name="rmsnorm",
    )(x, g.reshape(1, d))


def _mm_epilogue(r, s_ref, act, dtype):
    if s_ref is not None:
        r = r * s_ref[...]
    if act == "relu2":
        r = jnp.square(jnp.maximum(r, 0.0))
    return r.astype(dtype)


def _mm_kernel(*refs, nk, act, scaled):
    if scaled:
        a_ref, b_ref, s_ref, o_ref = refs[:4]
        rest = refs[4:]
    else:
        a_ref, b_ref, o_ref = refs[:3]
        s_ref = None
        rest = refs[3:]
    def partial_product():
        return jnp.dot(a_ref[...], b_ref[...], preferred_element_type=F32)

    if nk == 1:
        o_ref[...] = _mm_epilogue(partial_product(), s_ref, act, o_ref.dtype)
        return
    acc_ref, = rest
    k = pl.program_id(2)

    @pl.when(k == 0)
    def _():
        acc_ref[...] = partial_product()

    @pl.when(jnp.logical_and(k > 0, k < nk - 1))
    def _():
        acc_ref[...] += partial_product()

    @pl.when(k == nk - 1)
    def _():
        o_ref[...] = _mm_epilogue(acc_ref[...] + partial_product(), s_ref, act, o_ref.dtype)


def _cast_kernel(x_ref, o_ref):
    o_ref[...] = x_ref[...].astype(o_ref.dtype)


def _to_bf16(w, layer, tr=512, tc=4096):
    _, r, c = w.shape
    tc = min(tc, c)
    return pl.pallas_call(
        _cast_kernel,
        grid=(r // tr, c // tc),
        in_specs=[pl.BlockSpec((None, tr, tc), lambda i, j: (layer, i, j))],
        out_specs=pl.BlockSpec((tr, tc), lambda i, j: (i, j)),
        out_shape=jax.ShapeDtypeStruct((r, c), BF16),
        compiler_params=_cparams(("parallel", "parallel")),
        name="weight_to_bf16",
    )(w)


def _matmul(a, b, out_dtype, colscale=None, act=None, b_cols=None, tm=1024, tn=1024, tk=None):
    m, kdim = a.shape
    c0, n = b_cols if b_cols is not None else (0, b.shape[1])
    jb = c0 // tn
    assert c0 % tn == 0 and n % tn == 0 and m % tm == 0
    if tk is None:
        tk = kdim if kdim <= 4096 else 2048
    nk = kdim // tk
    scaled = colscale is not None
    in_specs = [pl.BlockSpec((tm, tk), lambda i, j, k: (i, k)),
                pl.BlockSpec((tk, tn), lambda i, j, k: (k, j + jb))]
    args = [a, b]
    if scaled:
        in_specs.append(pl.BlockSpec((1, tn), lambda i, j, k: (0, j)))
        args.append(colscale.reshape(1, n).astype(F32))
    scratch = [pltpu.VMEM((tm, tn), F32)] if nk > 1 else []
    return pl.pallas_call(
        functools.partial(_mm_kernel, nk=nk, act=act, scaled=scaled),
        grid=(m // tm, n // tn, nk),
        in_specs=in_specs,
        out_specs=pl.BlockSpec((tm, tn), lambda i, j, k: (i, j)),
        out_shape=jax.ShapeDtypeStruct((m, n), out_dtype),
        scratch_shapes=scratch,
        compiler_params=_cparams(("parallel", "parallel", "arbitrary")),
        name="matmul",
    )(*args)


def _norm_res_kernel(x_ref, y_ref, g_ref, *rest, with_next):
    y = y_ref[...]
    ms = jnp.mean(y * y, axis=-1, keepdims=True)
    xn = x_ref[...] + y * lax.rsqrt(ms + RMS_EPS) * g_ref[...]
    if with_next:
        g2_ref, o_ref, h_ref = rest
        o_ref[...] = xn
        ms2 = jnp.mean(xn * xn, axis=-1, keepdims=True)
        h_ref[...] = (xn * lax.rsqrt(ms2 + RMS_EPS) * g2_ref[...]).astype(h_ref.dtype)
    else:
        o_ref, = rest
        o_ref[...] = xn


def _norm_res(x, y, g, g_next=None, tm=128):
    m, d = x.shape
    with_next = g_next is not None
    row = pl.BlockSpec((tm, d), lambda i: (i, 0))
    vec = pl.BlockSpec((1, d), lambda i: (0, 0))
    in_specs = [row, row, vec]
    args = [x, y, g.reshape(1, d)]
    out_shape = jax.ShapeDtypeStruct((m, d), F32)
    out_specs = row
    if with_next:
        in_specs.append(vec)
        args.append(g_next.reshape(1, d))
        out_shape = (out_shape, jax.ShapeDtypeStruct((m, d), BF16))
        out_specs = (row, row)
    return pl.pallas_call(
        functools.partial(_norm_res_kernel, with_next=with_next),
        grid=(m // tm,),
        in_specs=in_specs,
        out_specs=out_specs,
        out_shape=out_shape,
        compiler_params=_cparams(("parallel",)),
        name="norm_res",
    )(*args)


def _t5_bucket_np(rel):
    half = NUM_BUCKETS // 2
    max_exact = half // 2
    ret = np.where(rel > 0, half, 0)
    n = np.abs(rel)
    nf = np.maximum(n, 1).astype(np.float32)
    large = max_exact + (np.log(nf / np.float32(max_exact)) / np.float32(math.log(MAX_DISTANCE / max_exact))
                         * np.float32(half - max_exact)).astype(np.int32)
    large = np.minimum(large, half - 1)
    return (ret + np.where(n < max_exact, n, large)).astype(np.int32)


def _bias_kernel(tab_ref, idx_ref, o_ref, *, col0, scale, rb, present):
    h = pl.program_id(0)
    for i, buckets in enumerate(present):
        idx = idx_ref[i * rb:(i + 1) * rb, :]
        acc = jnp.full(idx.shape, NEG_INF, F32) if -1 in buckets else None
        for b in buckets:
            if b < 0:
                continue
            val = tab_ref[b, col0 + h] * scale
            acc = jnp.full(idx.shape, val, F32) if acc is None else jnp.where(idx == b, val, acc)
        o_ref[0, i * rb:(i + 1) * rb, :] = acc


def _build_bias(rel_bias, idx, n_heads, col0, rb=128, scale=1.0):
    r, c = idx.shape
    present = tuple(tuple(int(b) for b in np.unique(idx[i:i + rb])) for i in range(0, r, rb))
    return pl.pallas_call(
        functools.partial(_bias_kernel, col0=col0, scale=scale, rb=rb, present=present),
        grid=(n_heads,),
        in_specs=[pl.BlockSpec(memory_space=pltpu.SMEM),
                  pl.BlockSpec((r, c), lambda h: (0, 0))],
        out_specs=pl.BlockSpec((1, r, c), lambda h: (h, 0, 0)),
        out_shape=jax.ShapeDtypeStruct((n_heads, r, c), F32),
        compiler_params=_cparams(("parallel",)),
        name="bias_tiles",
    )(rel_bias, jnp.asarray(idx))


def _dil_bias_idx():
    row = np.arange(DIL_QBLK)[:, None]
    col = np.arange(DIL_KWIN)[None, :]
    tiles = []
    for d in DILATIONS:
        for off in (0, -DIL_RADIUS, -2 * DIL_RADIUS):
            rel = off + col - row
            idx = _t5_bucket_np(rel * d)
            tiles.append(np.where(np.abs(rel) <= DIL_RADIUS, idx, -1))
    return np.concatenate(tiles, axis=0).astype(np.int32)


def _diff_bias_idx(seq):
    far = np.arange(DIFF_FAR, seq)
    assert np.all(_t5_bucket_np(-far) == NUM_BUCKETS // 2 - 1) and np.all(_t5_bucket_np(far) == NUM_BUCKETS - 1)
    row = np.arange(DIFF_BAND + 2 * DIFF_FAR)[:, None]
    col = np.arange(DIFF_QBLK)[None, :]
    return _t5_bucket_np(row - 2 * DIFF_FAR - col)


def _diff_key_steps(seq):
    key = np.arange(seq)[:, None]
    lane = np.arange(LANES)[None, :]
    return np.where((lane < 64) & (key >= LANES * (lane & 31)), 1.0, 0.0).astype(np.float32)


def _dil_kernel(q_ref, k_ref, v_ref, bias_ref, o_ref, m_scr, l_scr, n_scr, *, seq):
    n_it = seq // DIL_QBLK
    for bi, d in enumerate(DILATIONS):
        sub_len = seq // d
        nblk = sub_len // DIL_QBLK
        log_d = d.bit_length() - 1
        first = bi == 0
        last = bi == len(DILATIONS) - 1

        def rows(start, size, d=d):
            if d == 1:
                return pl.ds(pl.multiple_of(start, DIL_QBLK // 2), size)
            return pl.ds(start, size, stride=d)

        def body(grp, carry, d=d, nblk=nblk, log_d=log_d, first=first, last=last, bi=bi,
                 sub_len=sub_len, rows=rows):
            qsls, scores, vbs = [], [], []
            for u in range(DIL_UNROLL):
                it = grp * DIL_UNROLL + u
                r = it & (d - 1)
                j = it >> log_d
                q0 = j * DIL_QBLK
                ws = jnp.clip(q0 - DIL_RADIUS, 0, sub_len - DIL_KWIN)
                variant = jnp.where(j == 0, 0, jnp.where(j == nblk - 1, 2, 1))
                qsl = rows(r + d * q0, DIL_QBLK)
                ksl = rows(r + d * ws, DIL_KWIN)
                qb = q_ref[qsl, :].astype(BF16)
                kb = k_ref[ksl, :].astype(BF16)
                s = lax.dot_general(qb, kb, (((1,), (1,)), ((), ())), preferred_element_type=F32)
                qsls.append(qsl)
                scores.append(s + bias_ref[0, bi * 3 + variant])
                vbs.append(v_ref[ksl, :].astype(BF16))
            stats = []
            for s in scores:
                m = jnp.max(s, axis=1, keepdims=True)
                p = jnp.exp(s - m)
                stats.append((m, jnp.sum(p, axis=1, keepdims=True), p.astype(BF16)))
            nums = [jnp.dot(p, vb, preferred_element_type=F32) for (_, _, p), vb in zip(stats, vbs)]
            shape = (DIL_QBLK, LANES)
            for qsl, (m, l, _), n in zip(qsls, stats, nums):
                m = jnp.broadcast_to(m, shape)
                l = jnp.broadcast_to(l, shape)
                if not first:
                    m_old = m_scr[qsl, :]
                    m_new = jnp.maximum(m_old, m)
                    a = jnp.exp(m_old - m_new)
                    b = jnp.exp(m - m_new)
                    l = a * l_scr[qsl, :] + b * l
                    n = a * n_scr[qsl, :] + b * n
                    m = m_new
                if last:
                    o_ref[qsl, :] = (n / l).astype(o_ref.dtype)
                else:
                    m_scr[qsl, :] = m
                    l_scr[qsl, :] = l
                    n_scr[qsl, :] = n
            return carry

        lax.fori_loop(0, n_it // DIL_UNROLL, body, 0)


def _dil_attention(proj, bias_tiles, seq):
    bsz = proj.shape[0]
    nh = N_HEADS_DIL

    def col(off):
        return pl.BlockSpec((None, seq, HEAD_DIM), lambda b, h: (b, 0, off + h))

    state = pltpu.VMEM((seq, LANES), F32)
    return pl.pallas_call(
        functools.partial(_dil_kernel, seq=seq),
        grid=(bsz, nh),
        in_specs=[col(0), col(nh), col(2 * nh),
                  pl.BlockSpec((1, 3 * len(DILATIONS), DIL_QBLK, DIL_KWIN), lambda b, h: (h, 0, 0, 0))],
        out_specs=pl.BlockSpec((None, seq, HEAD_DIM), lambda b, h: (b, 0, h)),
        out_shape=jax.ShapeDtypeStruct((bsz, seq, DIL_WIDTH), BF16),
        scratch_shapes=[state, state, state],
        compiler_params=_cparams(("parallel", "parallel")),
        name="dilated_attention",
    )(proj, proj, proj, bias_tiles)


def _diff_kernel(tab_ref, lam_ref, q_ref, k_ref, ksteps_ref, vt_ref, t_ref, g_ref, o_ref, o_scr,
                 *, seq, lambda_init):
    h = pl.program_id(0)
    q0 = pl.program_id(2) * DIFF_QBLK
    ks = jnp.clip(q0 - DIFF_FAR, 0, seq - DIFF_BAND)
    ts = ks - q0 + 2 * DIFF_FAR
    step_lo = lax.shift_right_logical(ks, 7)
    step_hi = step_lo + DIFF_BAND // LANES
    n_piece = seq // LANES

    lam = (jnp.exp(jnp.sum(lam_ref[0:1, :] * lam_ref[1:2, :], axis=1, keepdims=True))
           - jnp.exp(jnp.sum(lam_ref[2:3, :] * lam_ref[3:4, :], axis=1, keepdims=True)) + lambda_init)

    c_left = tab_ref[NUM_BUCKETS // 2 - 1, N_HEADS_DIL + h] * LOG2E
    c_right = tab_ref[NUM_BUCKETS - 1, N_HEADS_DIL + h] * LOG2E
    lane = lax.broadcasted_iota(jnp.int32, (1, LANES), 1)
    lj = lane & 31
    c = (jnp.where(lj == 0, c_left, 0.0) + jnp.where(lj == step_lo, -c_left, 0.0)
         + jnp.where(lj == step_hi, c_right, 0.0))
    c = jnp.where(lane < 64, c, 0.0)
    c_hi = c.astype(BF16).astype(F32)
    q_extra = jnp.where(lane < 32, c_hi, c - c_hi).astype(BF16)
    q_extra = jnp.broadcast_to(q_extra, (DIFF_QBLK, LANES))

    q = q_ref[...]
    qlane = lax.broadcasted_iota(jnp.int32, q.shape, 1)
    zero = jnp.zeros_like(q)
    q_both = jnp.concatenate(
        [jnp.concatenate([jnp.where(qlane < DIFF_QK_DIM, q, zero), q_extra], axis=1),
         jnp.concatenate([jnp.where(qlane >= DIFF_QK_DIM, q, zero), q_extra], axis=1)], axis=0)

    pieces_per_chunk = DIFF_CHUNK // LANES

    def piece_start(g):
        return pl.multiple_of(((step_lo + g) & (n_piece - 1)) * LANES, LANES)

    def scores(g0):
        k_c = jnp.concatenate(
            [jnp.concatenate([k_ref[pl.ds(piece_start(g0 + i), LANES), :],
                              ksteps_ref[pl.ds(piece_start(g0 + i), LANES), :]], axis=1)
             for i in range(pieces_per_chunk)], axis=0)
        return lax.dot_general(k_c, q_both, (((1,), (1,)), ((), ())), preferred_element_type=F32)

    def band_bias(g0):
        tile = t_ref[0, pl.ds(pl.multiple_of(ts + g0 * LANES, LANES), DIFF_CHUNK), :]
        return jnp.concatenate([tile, tile], axis=1)

    def values_t(g0):
        return jnp.concatenate([vt_ref[:, pl.ds(piece_start(g0 + i), LANES)]
                                for i in range(pieces_per_chunk)], axis=1)

    def finish(o, l):
        on = o / l
        od = on[:, :DIFF_QBLK] - lam * on[:, DIFF_QBLK:]
        ms = jnp.mean(od * od, axis=0, keepdims=True)
        y = od * lax.rsqrt(ms + SUBLN_EPS) * g_ref[...] * (1.0 - lambda_init)
        o_ref[...] = y.T.astype(o_ref.dtype)

    chunk_starts = list(range(0, n_piece, pieces_per_chunk))
    width = 2 * DIFF_QBLK

    def fold8(x, op):
        return op(x.reshape(x.shape[0] // 8, 8, width), axis=0)

    s = scores(0)
    ref = jnp.max(s + band_bias(0), axis=0, keepdims=True)
    m8 = jnp.broadcast_to(ref, (8, width))
    l8 = jnp.zeros((8, width), F32)
    o = None
    for ci, g0 in enumerate(chunk_starts):
        if ci + 1 < len(chunk_starts):
            s_next = scores(chunk_starts[ci + 1])
        p_slabs = []
        for r0 in range(0, DIFF_CHUNK, DIFF_SLAB):
            slab = s[r0:r0 + DIFF_SLAB]
            if g0 * LANES < DIFF_BAND:
                tile = t_ref[0, pl.ds(pl.multiple_of(ts + g0 * LANES + r0, DIFF_SLAB), DIFF_SLAB), :]
                slab = slab + jnp.concatenate([tile, tile], axis=1)
            m8 = jnp.maximum(m8, fold8(slab, jnp.max))
            p = jnp.exp2(slab - ref)
            l8 = l8 + fold8(p, jnp.sum)
            p_slabs.append(p.astype(BF16))
        o_c = jnp.dot(values_t(g0), jnp.concatenate(p_slabs, axis=0), preferred_element_type=F32)
        o = o_c if o is None else o + o_c
        s = s_next
    finish(o, jnp.sum(l8, axis=0, keepdims=True))

    @pl.when(jnp.max(m8 - ref) > DIFF_SAFE_GAP)
    def _():
        def chunk(c, carry, in_band):
            m_old, l_old = carry
            g0 = c * pieces_per_chunk
            sc = scores(g0)
            if in_band:
                sc = sc + band_bias(g0)
            m_new = jnp.maximum(m_old, jnp.max(sc, axis=0, keepdims=True))
            alpha = jnp.exp2(m_old - m_new)
            pc = jnp.exp2(sc - m_new)
            o_scr[...] = alpha * o_scr[...] + jnp.dot(values_t(g0), pc.astype(BF16), preferred_element_type=F32)
            return m_new, alpha * l_old + jnp.sum(pc, axis=0, keepdims=True)

        o_scr[...] = jnp.zeros_like(o_scr)
        carry = (jnp.full((1, 2 * DIFF_QBLK), NEG_INF, F32), jnp.zeros((1, 2 * DIFF_QBLK), F32))
        n_band = DIFF_BAND // DIFF_CHUNK
        carry = lax.fori_loop(0, n_band, functools.partial(chunk, in_band=True), carry)
        _, l_slow = lax.fori_loop(n_band, len(chunk_starts), functools.partial(chunk, in_band=False), carry)
        finish(o_scr[...], l_slow)


def _diff_attention(rel_bias, proj, vt, t_bias, lam_vecs, g_subln, seq, lambda_init):
    bsz = proj.shape[0]
    nh = N_HEADS_DIFF
    nq = seq // DIFF_QBLK
    assert seq == 32 * LANES and DIFF_BAND % DIFF_CHUNK == 0 and seq % DIFF_CHUNK == 0
    ksteps = jnp.asarray(_diff_key_steps(seq), BF16)
    return pl.pallas_call(
        functools.partial(_diff_kernel, seq=seq, lambda_init=lambda_init),
        grid=(nh, bsz, nq),
        in_specs=[pl.BlockSpec(memory_space=pltpu.SMEM),
                  pl.BlockSpec((4, DIFF_QK_DIM), lambda h, b, i: (0, 0)),
                  pl.BlockSpec((None, DIFF_QBLK, HEAD_DIM), lambda h, b, i: (b, i, h)),
                  pl.BlockSpec((None, seq, HEAD_DIM), lambda h, b, i: (b, 0, nh + h)),
                  pl.BlockSpec((seq, LANES), lambda h, b, i: (0, 0)),
                  pl.BlockSpec((None, None, HEAD_DIM, seq), lambda h, b, i: (b, h, 0, 0)),
                  pl.BlockSpec((1, DIFF_BAND + 2 * DIFF_FAR, DIFF_QBLK), lambda h, b, i: (h, 0, 0)),
                  pl.BlockSpec((HEAD_DIM, 1), lambda h, b, i: (0, 0))],
        out_specs=pl.BlockSpec((None, DIFF_QBLK, HEAD_DIM), lambda h, b, i: (b, i, h)),
        out_shape=jax.ShapeDtypeStruct((bsz, seq, DIFF_WIDTH), BF16),
        scratch_shapes=[pltpu.VMEM((HEAD_DIM, 2 * DIFF_QBLK), F32)],
        compiler_params=_cparams(("parallel", "parallel", "parallel")),
        name="diff_attention",
    )(rel_bias, lam_vecs, proj, proj, ksteps, vt, t_bias, g_subln.reshape(HEAD_DIM, 1))


def kernel(x, rel_bias, g_pre_mix, w_in, lambda_q1, lambda_k1, lambda_q2, lambda_k2, g_subln, w_out,
           g_post_mix, g_pre_mlp, w_up, w_down, g_post_mlp):
    bsz, seq, d = x.shape
    m = bsz * seq
    xf = x.reshape(m, d)

    dil_tiles = _build_bias(rel_bias, _dil_bias_idx(), N_HEADS_DIL, 0)
    dil_tiles = dil_tiles.reshape(N_HEADS_DIL, 3 * len(DILATIONS), DIL_QBLK, DIL_KWIN)
    diff_t = _build_bias(rel_bias, _diff_bias_idx(seq), N_HEADS_DIFF, N_HEADS_DIL, scale=LOG2E)

    ones = jnp.ones((DIL_WIDTH,), F32)
    scale_dil = jnp.concatenate([ones * (HEAD_DIM ** -0.5), ones, ones])
    scale_diff = jnp.concatenate([ones * (DIFF_QK_DIM ** -0.5 * LOG2E), ones, ones])

    h = _rmsnorm(xf, g_pre_mix[0])
    for l in range(DEPTH):
        lambda_init = 0.8 - 0.6 * math.exp(-0.3 * l)
        w_in_l = _to_bf16(w_in, l)
        proj_a = _matmul(h, w_in_l, F32, colscale=scale_dil, b_cols=(0, 3 * DIL_WIDTH))
        proj_d = _matmul(h, w_in_l, BF16, colscale=scale_diff, b_cols=(3 * DIL_WIDTH, 3 * DIFF_WIDTH))
        out_a = _dil_attention(proj_a.reshape(bsz, seq, 3 * DIL_WIDTH), dil_tiles, seq)
        lam_vecs = jnp.stack([lambda_q1[l], lambda_k1[l], lambda_q2[l], lambda_k2[l]]).astype(F32)
        proj_d = proj_d.reshape(bsz, seq, 3 * DIFF_WIDTH)
        vt = proj_d[:, :, 2 * DIFF_WIDTH:].reshape(bsz, seq, N_HEADS_DIFF, HEAD_DIM).transpose(0, 2, 3, 1)
        out_d = _diff_attention(rel_bias, proj_d, vt, diff_t, lam_vecs, g_subln[l], seq, lambda_init)
        mixed = jnp.concatenate([out_a, out_d], axis=-1).reshape(m, DIL_WIDTH + DIFF_WIDTH)
        y = _matmul(mixed, _to_bf16(w_out, l), F32)
        xf, h = _norm_res(xf, y, g_post_mix[l], g_pre_mlp[l])
        u = _matmul(h, _to_bf16(w_up, l), BF16, act="relu2")
        y = _matmul(u, _to_bf16(w_down, l), F32)
        if l + 1 < DEPTH:
            xf, h = _norm_res(xf, y, g_post_mlp[l], g_pre_mix[l + 1])
        else:
            xf = _norm_res(xf, y, g_post_mlp[l])
    return xf.reshape(bsz, seq, d)
```

```python
import functools
import math

import numpy as np
import jax
import jax.numpy as jnp
from jax import lax
from jax.experimental import pallas as pl
from jax.experimental.pallas import tpu as pltpu

D_MODEL = 4096
DEPTH = 2
HEAD_DIM = 128
N_HEADS_DIL = 16
N_HEADS_DIFF = 16
DIFF_QK_DIM = 64
DIL_WIDTH = N_HEADS_DIL * HEAD_DIM
DIFF_WIDTH = N_HEADS_DIFF * HEAD_DIM
DILATIONS = (16, 4, 1)
DIL_RADIUS = 64
NUM_BUCKETS = 32
MAX_DISTANCE = 1024
RMS_EPS = 1e-6
SUBLN_EPS = 1e-5
NEG_INF = -1e30

LANES = 128
DIL_QBLK = 128
DIL_KWIN = 256
DIL_UNROLL = 8
DIFF_QBLK = 512
DIFF_FAR = 768
DIFF_BAND = 2 * DIFF_FAR + DIFF_QBLK
DIFF_CHUNK = 512
DIFF_SLAB = 512
DIFF_SAFE_GAP = 100.0
LOG2E = 1.4426950408889634
VMEM_LIMIT = 56 * 1024 * 1024

F32 = jnp.float32
BF16 = jnp.bfloat16


def _cparams(sem, flags=None):
    return pltpu.CompilerParams(dimension_semantics=sem, vmem_limit_bytes=VMEM_LIMIT, flags=flags)


def _rmsnorm_kernel(x_ref, g_ref, o_ref):
    x = x_ref[...]
    ms = jnp.mean(x * x, axis=-1, keepdims=True)
    o_ref[...] = (x * lax.rsqrt(ms + RMS_EPS) * g_ref[...]).astype(o_ref.dtype)


def _rmsnorm(x, g, tm=256):
    m, d = x.shape
    return pl.pallas_call(
        _rmsnorm_kernel,
        grid=(m // tm,),
        in_specs=[pl.BlockSpec((tm, d), lambda i: (i, 0)),
                  pl.BlockSpec((1, d), lambda i: (0, 0))],
        out_specs=pl.BlockSpec((tm, d), lambda i: (i, 0)),
        out_shape=jax.ShapeDtypeStruct((m, d), BF16),
        compiler_params=_cparams(("parallel",)),
        name="rmsnorm",
    )(x, g.reshape(1, d))


def _mm_epilogue(r, s_ref, act, dtype):
    if s_ref is not None:
        r = r * s_ref[...]
    if act == "relu2":
        r = jnp.square(jnp.maximum(r, 0.0))
    return r.astype(dtype)


def _mm_kernel(*refs, nk, act, scaled):
    if scaled:
        a_ref, b_ref, s_ref, o_ref = refs[:4]
        rest = refs[4:]
    else:
        a_ref, b_ref, o_ref = refs[:3]
        s_ref = None
        rest = refs[3:]
    def partial_product():
        return jnp.dot(a_ref[...], b_ref[...].astype(BF16), preferred_element_type=F32)

    if nk == 1:
        o_ref[...] = _mm_epilogue(partial_product(), s_ref, act, o_ref.dtype)
        return
    acc_ref, = rest
    k = pl.program_id(2)

    @pl.when(k == 0)
    def _():
        acc_ref[...] = partial_product()

    @pl.when(jnp.logical_and(k > 0, k < nk - 1))
    def _():
        acc_ref[...] += partial_product()

    @pl.when(k == nk - 1)
    def _():
        o_ref[...] = _mm_epilogue(acc_ref[...] + partial_product(), s_ref, act, o_ref.dtype)


def _matmul(a, w, layer, out_dtype, colscale=None, act=None, b_cols=None, tm=2048, tn=1024, tk=1024):
    m, kdim = a.shape
    c0, n = b_cols if b_cols is not None else (0, w.shape[2])
    jb = c0 // tn
    assert c0 % tn == 0 and n % tn == 0 and m % tm == 0 and kdim % tk == 0
    nk = kdim // tk
    scaled = colscale is not None
    in_specs = [pl.BlockSpec((tm, tk), lambda i, j, k: (i, k)),
                pl.BlockSpec((None, tk, tn), lambda i, j, k: (layer, k, j + jb))]
    args = [a, w]
    if scaled:
        in_specs.append(pl.BlockSpec((1, tn), lambda i, j, k: (0, j)))
        args.append(colscale.reshape(1, n).astype(F32))
    scratch = [pltpu.VMEM((tm, tn), F32)] if nk > 1 else []
    return pl.pallas_call(
        functools.partial(_mm_kernel, nk=nk, act=act, scaled=scaled),
        grid=(m // tm, n // tn, nk),
        in_specs=in_specs,
        out_specs=pl.BlockSpec((tm, tn), lambda i, j, k: (i, j)),
        out_shape=jax.ShapeDtypeStruct((m, n), out_dtype),
        scratch_shapes=scratch,
        compiler_params=_cparams(("parallel", "parallel", "arbitrary")),
        name="matmul",
    )(*args)


def _norm_res_kernel(x_ref, y_ref, g_ref, *rest, with_next):
    y = y_ref[...]
    ms = jnp.mean(y * y, axis=-1, keepdims=True)
    xn = x_ref[...] + y * lax.rsqrt(ms + RMS_EPS) * g_ref[...]
    if with_next:
        g2_ref, o_ref, h_ref = rest
        o_ref[...] = xn
        ms2 = jnp.mean(xn * xn, axis=-1, keepdims=True)
        h_ref[...] = (xn * lax.rsqrt(ms2 + RMS_EPS) * g2_ref[...]).astype(h_ref.dtype)
    else:
        o_ref, = rest
        o_ref[...] = xn


def _norm_res(x, y, g, g_next=None, tm=128):
    m, d = x.shape
    with_next = g_next is not None
    row = pl.BlockSpec((tm, d), lambda i: (i, 0))
    vec = pl.BlockSpec((1, d), lambda i: (0, 0))
    in_specs = [row, row, vec]
    args = [x, y, g.reshape(1, d)]
    out_shape = jax.ShapeDtypeStruct((m, d), F32)
    out_specs = row
    if with_next:
        in_specs.append(vec)
        args.append(g_next.reshape(1, d))
        out_shape = (out_shape, jax.ShapeDtypeStruct((m, d), BF16))
        out_specs = (row, row)
    return pl.pallas_call(
        functools.partial(_norm_res_kernel, with_next=with_next),
        grid=(m // tm,),
        in_specs=in_specs,
        out_specs=out_specs,
        out_shape=out_shape,
        compiler_params=_cparams(("parallel",)),
        name="norm_res",
    )(*args)


def _t5_bucket_np(rel):
    half = NUM_BUCKETS // 2
    max_exact = half // 2
    ret = np.where(rel > 0, half, 0)
    n = np.abs(rel)
    nf = np.maximum(n, 1).astype(np.float32)
    large = max_exact + (np.log(nf / np.float32(max_exact)) / np.float32(math.log(MAX_DISTANCE / max_exact))
                         * np.float32(half - max_exact)).astype(np.int32)
    large = np.minimum(large, half - 1)
    return (ret + np.where(n < max_exact, n, large)).astype(np.int32)


def _bias_kernel(tab_ref, idx_ref, o_ref, *, col0, scale, rb, present):
    h = pl.program_id(0)
    for i, buckets in enumerate(present):
        idx = idx_ref[i * rb:(i + 1) * rb, :]
        acc = jnp.full(idx.shape, NEG_INF, F32) if -1 in buckets else None
        for b in buckets:
            if b < 0:
                continue
            val = tab_ref[b, col0 + h] * scale
            acc = jnp.full(idx.shape, val, F32) if acc is None else jnp.where(idx == b, val, acc)
        o_ref[0, i * rb:(i + 1) * rb, :] = acc


def _build_bias(rel_bias, idx, n_heads, col0, rb=128, scale=1.0):
    r, c = idx.shape
    present = tuple(tuple(int(b) for b in np.unique(idx[i:i + rb])) for i in range(0, r, rb))
    return pl.pallas_call(
        functools.partial(_bias_kernel, col0=col0, scale=scale, rb=rb, present=present),
        grid=(n_heads,),
        in_specs=[pl.BlockSpec(memory_space=pltpu.SMEM),
                  pl.BlockSpec((r, c), lambda h: (0, 0))],
        out_specs=pl.BlockSpec((1, r, c), lambda h: (h, 0, 0)),
        out_shape=jax.ShapeDtypeStruct((n_heads, r, c), F32),
        compiler_params=_cparams(("parallel",)),
        name="bias_tiles",
    )(rel_bias, jnp.asarray(idx))


def _dil_bias_idx():
    row = np.arange(DIL_QBLK)[:, None]
    col = np.arange(DIL_KWIN)[None, :]
    tiles = []
    for d in DILATIONS:
        for off in (0, -DIL_RADIUS, -2 * DIL_RADIUS):
            rel = off + col - row
            idx = _t5_bucket_np(rel * d)
            tiles.append(np.where(np.abs(rel) <= DIL_RADIUS, idx, -1))
    return np.concatenate(tiles, axis=0).astype(np.int32)


def _diff_bias_idx(seq):
    far = np.arange(DIFF_FAR, seq)
    assert np.all(_t5_bucket_np(-far) == NUM_BUCKETS // 2 - 1) and np.all(_t5_bucket_np(far) == NUM_BUCKETS - 1)
    row = np.arange(DIFF_BAND + 2 * DIFF_FAR)[:, None]
    col = np.arange(DIFF_QBLK)[None, :]
    return _t5_bucket_np(row - 2 * DIFF_FAR - col)


def _diff_key_steps(seq):
    key = np.arange(seq)[:, None]
    lane = np.arange(LANES)[None, :]
    return np.where((lane < 64) & (key >= LANES * (lane & 31)), 1.0, 0.0).astype(np.float32)


def _dil_kernel(q_ref, k_ref, v_ref, bias_ref, o_ref, m_scr, l_scr, n_scr, *, seq):
    n_it = seq // DIL_QBLK
    for bi, d in enumerate(DILATIONS):
        sub_len = seq // d
        nblk = sub_len // DIL_QBLK
        log_d = d.bit_length() - 1
        first = bi == 0
        last = bi == len(DILATIONS) - 1

        def rows(start, size, d=d):
            if d == 1:
                return pl.ds(pl.multiple_of(start, DIL_QBLK // 2), size)
            return pl.ds(start, size, stride=d)

        def body(grp, carry, d=d, nblk=nblk, log_d=log_d, first=first, last=last, bi=bi,
                 sub_len=sub_len, rows=rows):
            qsls, scores, vbs = [], [], []
            for u in range(DIL_UNROLL):
                it = grp * DIL_UNROLL + u
                r = it & (d - 1)
                j = it >> log_d
                q0 = j * DIL_QBLK
                ws = jnp.clip(q0 - DIL_RADIUS, 0, sub_len - DIL_KWIN)
                variant = jnp.where(j == 0, 0, jnp.where(j == nblk - 1, 2, 1))
                qsl = rows(r + d * q0, DIL_QBLK)
                ksl = rows(r + d * ws, DIL_KWIN)
                qb = q_ref[qsl, :].astype(BF16)
                kb = k_ref[ksl, :].astype(BF16)
                s = lax.dot_general(qb, kb, (((1,), (1,)), ((), ())), preferred_element_type=F32)
                qsls.append(qsl)
                scores.append(s + bias_ref[0, bi * 3 + variant])
                vbs.append(v_ref[ksl, :].astype(BF16))
            stats = []
            for s in scores:
                m = jnp.max(s, axis=1, keepdims=True)
                p = jnp.exp(s - m)
                stats.append((m, jnp.sum(p, axis=1, keepdims=True), p.astype(BF16)))
            nums = [jnp.dot(p, vb, preferred_element_type=F32) for (_, _, p), vb in zip(stats, vbs)]
            shape = (DIL_QBLK, LANES)
            for qsl, (m, l, _), n in zip(qsls, stats, nums):
                m = jnp.broadcast_to(m, shape)
                l = jnp.broadcast_to(l, shape)
                if not first:
                    m_old = m_scr[qsl, :]
                    m_new = jnp.maximum(m_old, m)
                    a = jnp.exp(m_old - m_new)
                    b = jnp.exp(m - m_new)
                    l = a * l_scr[qsl, :] + b * l
                    n = a * n_scr[qsl, :] + b * n
                    m = m_new
                if last:
                    o_ref[qsl, :] = (n / l).astype(o_ref.dtype)
                else:
                    m_scr[qsl, :] = m
                    l_scr[qsl, :] = l
                    n_scr[qsl, :] = n
            return carry

        lax.fori_loop(0, n_it // DIL_UNROLL, body, 0)


def _dil_attention(proj, bias_tiles, seq):
    bsz = proj.shape[0]
    nh = N_HEADS_DIL

    def col(off):
        return pl.BlockSpec((None, seq, HEAD_DIM), lambda b, h: (b, 0, off + h))

    state = pltpu.VMEM((seq, LANES), F32)
    return pl.pallas_call(
        functools.partial(_dil_kernel, seq=seq),
        grid=(bsz, nh),
        in_specs=[col(0), col(nh), col(2 * nh),
                  pl.BlockSpec((1, 3 * len(DILATIONS), DIL_QBLK, DIL_KWIN), lambda b, h: (h, 0, 0, 0))],
        out_specs=pl.BlockSpec((None, seq, HEAD_DIM), lambda b, h: (b, 0, h)),
        out_shape=jax.ShapeDtypeStruct((bsz, seq, DIL_WIDTH), BF16),
        scratch_shapes=[state, state, state],
        compiler_params=_cparams(("parallel", "parallel")),
        name="dilated_attention",
    )(proj, proj, proj, bias_tiles)


def _diff_kernel(tab_ref, lam_ref, q_ref, k_ref, ksteps_ref, vt_ref, t_ref, g_ref, o_ref, o_scr,
                 *, seq, lambda_init):
    h = pl.program_id(0)
    q0 = pl.program_id(2) * DIFF_QBLK
    ks = jnp.clip(q0 - DIFF_FAR, 0, seq - DIFF_BAND)
    ts = ks - q0 + 2 * DIFF_FAR
    step_lo = lax.shift_right_logical(ks, 7)
    step_hi = step_lo + DIFF_BAND // LANES
    n_piece = seq // LANES

    lam = (jnp.exp(jnp.sum(lam_ref[0:1, :] * lam_ref[1:2, :], axis=1, keepdims=True))
           - jnp.exp(jnp.sum(lam_ref[2:3, :] * lam_ref[3:4, :], axis=1, keepdims=True)) + lambda_init)

    c_left = tab_ref[NUM_BUCKETS // 2 - 1, N_HEADS_DIL + h] * LOG2E
    c_right = tab_ref[NUM_BUCKETS - 1, N_HEADS_DIL + h] * LOG2E
    lane = lax.broadcasted_iota(jnp.int32, (1, LANES), 1)
    lj = lane & 31
    c = (jnp.where(lj == 0, c_left, 0.0) + jnp.where(lj == step_lo, -c_left, 0.0)
         + jnp.where(lj == step_hi, c_right, 0.0))
    c = jnp.where(lane < 64, c, 0.0)
    c_hi = c.astype(BF16).astype(F32)
    q_extra = jnp.where(lane < 32, c_hi, c - c_hi).astype(BF16)
    q_extra = jnp.broadcast_to(q_extra, (DIFF_QBLK, LANES))

    q = q_ref[...]
    qlane = lax.broadcasted_iota(jnp.int32, q.shape, 1)
    zero = jnp.zeros_like(q)
    q_both = jnp.concatenate(
        [jnp.concatenate([jnp.where(qlane < DIFF_QK_DIM, q, zero), q_extra], axis=1),
         jnp.concatenate([jnp.where(qlane >= DIFF_QK_DIM, q, zero), q_extra], axis=1)], axis=0)

    pieces_per_chunk = DIFF_CHUNK // LANES

    def piece_start(g):
        return pl.multiple_of(((step_lo + g) & (n_piece - 1)) * LANES, LANES)

    def scores(g0):
        k_c = jnp.concatenate(
            [jnp.concatenate([k_ref[pl.ds(piece_start(g0 + i), LANES), :],
                              ksteps_ref[pl.ds(piece_start(g0 + i), LANES), :]], axis=1)
             for i in range(pieces_per_chunk)], axis=0)
        return lax.dot_general(k_c, q_both, (((1,), (1,)), ((), ())), preferred_element_type=F32)

    def band_bias(g0):
        tile = t_ref[0, pl.ds(pl.multiple_of(ts + g0 * LANES, LANES), DIFF_CHUNK), :]
        return jnp.concatenate([tile, tile], axis=1)

    def values_t(g0):
        return jnp.concatenate([vt_ref[:, pl.ds(piece_start(g0 + i), LANES)]
                                for i in range(pieces_per_chunk)], axis=1)

    def finish(o, l):
        on = o / l
        od = on[:, :DIFF_QBLK] - lam * on[:, DIFF_QBLK:]
        ms = jnp.mean(od * od, axis=0, keepdims=True)
        y = od * lax.rsqrt(ms + SUBLN_EPS) * g_ref[...] * (1.0 - lambda_init)
        o_ref[...] = y.T.astype(o_ref.dtype)

    chunk_starts = list(range(0, n_piece, pieces_per_chunk))
    width = 2 * DIFF_QBLK

    def fold8(x, op):
        return op(x.reshape(x.shape[0] // 8, 8, width), axis=0)

    s = scores(0)
    ref = jnp.max(s + band_bias(0), axis=0, keepdims=True)
    m8 = jnp.broadcast_to(ref, (8, width))
    l8 = jnp.zeros((8, width), F32)
    o = None
    for ci, g0 in enumerate(chunk_starts):
        if ci + 1 < len(chunk_starts):
            s_next = scores(chunk_starts[ci + 1])
        p_slabs = []
        for r0 in range(0, DIFF_CHUNK, DIFF_SLAB):
            slab = s[r0:r0 + DIFF_SLAB]
            if g0 * LANES < DIFF_BAND:
                tile = t_ref[0, pl.ds(pl.multiple_of(ts + g0 * LANES + r0, DIFF_SLAB), DIFF_SLAB), :]
                slab = slab + jnp.concatenate([tile, tile], axis=1)
            m8 = jnp.maximum(m8, fold8(slab, jnp.max))
            p = jnp.exp2(slab - ref)
            l8 = l8 + fold8(p, jnp.sum)
            p_slabs.append(p.astype(BF16))
        o_c = jnp.dot(values_t(g0), jnp.concatenate(p_slabs, axis=0), preferred_element_type=F32)
        o = o_c if o is None else o + o_c
        s = s_next
    finish(o, jnp.sum(l8, axis=0, keepdims=True))

    @pl.when(jnp.max(m8 - ref) > DIFF_SAFE_GAP)
    def _():
        def chunk(c, carry, in_band):
            m_old, l_old = carry
            g0 = c * pieces_per_chunk
            sc = scores(g0)
            if in_band:
                sc = sc + band_bias(g0)
            m_new = jnp.maximum(m_old, jnp.max(sc, axis=0, keepdims=True))
            alpha = jnp.exp2(m_old - m_new)
            pc = jnp.exp2(sc - m_new)
            o_scr[...] = alpha * o_scr[...] + jnp.dot(values_t(g0), pc.astype(BF16), preferred_element_type=F32)
            return m_new, alpha * l_old + jnp.sum(pc, axis=0, keepdims=True)

        o_scr[...] = jnp.zeros_like(o_scr)
        carry = (jnp.full((1, 2 * DIFF_QBLK), NEG_INF, F32), jnp.zeros((1, 2 * DIFF_QBLK), F32))
        n_band = DIFF_BAND // DIFF_CHUNK
        carry = lax.fori_loop(0, n_band, functools.partial(chunk, in_band=True), carry)
        _, l_slow = lax.fori_loop(n_band, len(chunk_starts), functools.partial(chunk, in_band=False), carry)
        finish(o_scr[...], l_slow)


def _diff_attention(rel_bias, proj, vt, t_bias, lam_vecs, g_subln, seq, lambda_init):
    bsz = proj.shape[0]
    nh = N_HEADS_DIFF
    nq = seq // DIFF_QBLK
    assert seq == 32 * LANES and DIFF_BAND % DIFF_CHUNK == 0 and seq % DIFF_CHUNK == 0
    ksteps = jnp.asarray(_diff_key_steps(seq), BF16)
    return pl.pallas_call(
        functools.partial(_diff_kernel, seq=seq, lambda_init=lambda_init),
        grid=(nh, bsz, nq),
        in_specs=[pl.BlockSpec(memory_space=pltpu.SMEM),
                  pl.BlockSpec((4, DIFF_QK_DIM), lambda h, b, i: (0, 0)),
                  pl.BlockSpec((None, DIFF_QBLK, HEAD_DIM), lambda h, b, i: (b, i, h)),
                  pl.BlockSpec((None, seq, HEAD_DIM), lambda h, b, i: (b, 0, nh + h)),
                  pl.BlockSpec((seq, LANES), lambda h, b, i: (0, 0)),
                  pl.BlockSpec((None, None, HEAD_DIM, seq), lambda h, b, i: (b, h, 0, 0)),
                  pl.BlockSpec((1, DIFF_BAND + 2 * DIFF_FAR, DIFF_QBLK), lambda h, b, i: (h, 0, 0)),
                  pl.BlockSpec((HEAD_DIM, 1), lambda h, b, i: (0, 0))],
        out_specs=pl.BlockSpec((None, DIFF_QBLK, HEAD_DIM), lambda h, b, i: (b, i, h)),
        out_shape=jax.ShapeDtypeStruct((bsz, seq, DIFF_WIDTH), BF16),
        scratch_shapes=[pltpu.VMEM((HEAD_DIM, 2 * DIFF_QBLK), F32)],
        compiler_params=_cparams(("parallel", "parallel", "parallel")),
        name="diff_attention",
    )(rel_bias, lam_vecs, proj, proj, ksteps, vt, t_bias, g_subln.reshape(HEAD_DIM, 1))


def kernel(x, rel_bias, g_pre_mix, w_in, lambda_q1, lambda_k1, lambda_q2, lambda_k2, g_subln, w_out,
           g_post_mix, g_pre_mlp, w_up, w_down, g_post_mlp):
    bsz, seq, d = x.shape
    m = bsz * seq
    xf = x.reshape(m, d)

    dil_tiles = _build_bias(rel_bias, _dil_bias_idx(), N_HEADS_DIL, 0)
    dil_tiles = dil_tiles.reshape(N_HEADS_DIL, 3 * len(DILATIONS), DIL_QBLK, DIL_KWIN)
    diff_t = _build_bias(rel_bias, _diff_bias_idx(seq), N_HEADS_DIFF, N_HEADS_DIL, scale=LOG2E)

    ones = jnp.ones((DIL_WIDTH,), F32)
    scale_dil = jnp.concatenate([ones * (HEAD_DIM ** -0.5), ones, ones])
    scale_diff = jnp.concatenate([ones * (DIFF_QK_DIM ** -0.5 * LOG2E), ones, ones])

    h = _rmsnorm(xf, g_pre_mix[0])
    for l in range(DEPTH):
        lambda_init = 0.8 - 0.6 * math.exp(-0.3 * l)
        proj_a = _matmul(h, w_in, l, F32, colscale=scale_dil, b_cols=(0, 3 * DIL_WIDTH))
        proj_d = _matmul(h, w_in, l, BF16, colscale=scale_diff, b_cols=(3 * DIL_WIDTH, 3 * DIFF_WIDTH))
        out_a = _dil_attention(proj_a.reshape(bsz, seq, 3 * DIL_WIDTH), dil_tiles, seq)
        lam_vecs = jnp.stack([lambda_q1[l], lambda_k1[l], lambda_q2[l], lambda_k2[l]]).astype(F32)
        proj_d = proj_d.reshape(bsz, seq, 3 * DIFF_WIDTH)
        vt = proj_d[:, :, 2 * DIFF_WIDTH:].reshape(bsz, seq, N_HEADS_DIFF, HEAD_DIM).transpose(0, 2, 3, 1)
        out_d = _diff_attention(rel_bias, proj_d, vt, diff_t, lam_vecs, g_subln[l], seq, lambda_init)
        mixed = jnp.concatenate([out_a, out_d], axis=-1).reshape(m, DIL_WIDTH + DIFF_WIDTH)
        y = _matmul(mixed, w_out, l, F32)
        xf, h = _norm_res(xf, y, g_post_mix[l], g_pre_mlp[l])
        u = _matmul(h, w_up, l, BF16, act="relu2")
        y = _matmul(u, w_down, l, F32)
        if l + 1 < DEPTH:
            xf, h = _norm_res(xf, y, g_post_mlp[l], g_pre_mix[l + 1])
        else:
            xf = _norm_res(xf, y, g_post_mlp[l])
    return xf.reshape(bsz, seq, d)
```

```python
import functools
import math

import numpy as np
import jax
import jax.numpy as jnp
from jax import lax
from jax.experimental import pallas as pl
from jax.experimental.pallas import tpu as pltpu

D_MODEL = 4096
DEPTH = 2
HEAD_DIM = 128
N_HEADS_DIL = 16
N_HEADS_DIFF = 16
DIFF_QK_DIM = 64
DIL_WIDTH = N_HEADS_DIL * HEAD_DIM
DIFF_WIDTH = N_HEADS_DIFF * HEAD_DIM
DILATIONS = (16, 4, 1)
DIL_RADIUS = 64
NUM_BUCKETS = 32
MAX_DISTANCE = 1024
RMS_EPS = 1e-6
SUBLN_EPS = 1e-5
NEG_INF = -1e30

LANES = 128
DIL_QBLK = 128
DIL_KWIN = 256
DIL_UNROLL = 8
DIFF_QBLK = 512
DIFF_FAR = 768
DIFF_BAND = 2 * DIFF_FAR + DIFF_QBLK
DIFF_CHUNK = 512
DIFF_MAX_DENOM = 2.0 ** 80
LOG2E = 1.4426950408889634
VMEM_LIMIT = 56 * 1024 * 1024

F32 = jnp.float32
BF16 = jnp.bfloat16


def _cparams(sem, flags=None):
    return pltpu.CompilerParams(dimension_semantics=sem, vmem_limit_bytes=VMEM_LIMIT, flags=flags)


def _rmsnorm_kernel(x_ref, g_ref, o_ref):
    x = x_ref[...]
    ms = jnp.mean(x * x, axis=-1, keepdims=True)
    o_ref[...] = (x * lax.rsqrt(ms + RMS_EPS) * g_ref[...]).astype(o_ref.dtype)


def _rmsnorm(x, g, tm=256):
    m, d = x.shape
    return pl.pallas_call(
        _rmsnorm_kernel,
        grid=(m // tm,),
        in_specs=[pl.BlockSpec((tm, d), lambda i: (i, 0)),
                  pl.BlockSpec((1, d), lambda i: (0, 0))],
        out_specs=pl.BlockSpec((tm, d), lambda i: (i, 0)),
        out_shape=jax.ShapeDtypeStruct((m, d), BF16),
        compiler_params=_cparams(("parallel",)),
        name="rmsnorm",
    )(x, g.reshape(1, d))


def _mm_epilogue(r, s_ref, act, dtype):
    if s_ref is not None:
        r = r * s_ref[...]
    if act == "relu2":
        r = jnp.square(jnp.maximum(r, 0.0))
    return r.astype(dtype)


def _mm_kernel(*refs, nk, act, scaled, split_a):
    n_a = 2 if split_a else 1
    a_refs = refs[:n_a]
    b_ref = refs[n_a]
    s_ref = refs[n_a + 1] if scaled else None
    o_ref = refs[n_a + 1 + scaled]
    rest = refs[n_a + 2 + scaled:]

    def partial_product(a_ref):
        return jnp.dot(a_ref[...], b_ref[...].astype(BF16), preferred_element_type=F32)

    if nk == 1:
        o_ref[...] = _mm_epilogue(partial_product(a_refs[0]), s_ref, act, o_ref.dtype)
        return
    acc_ref = rest[0] if rest else o_ref
    k = pl.program_id(2)

    @pl.when(k == 0)
    def _():
        acc_ref[...] = partial_product(a_refs[0])

    @pl.when(jnp.logical_and(k > 0, k < nk - 1))
    def _():
        acc_ref[...] += partial_product(a_refs[0])

    @pl.when(k == nk - 1)
    def _():
        o_ref[...] = _mm_epilogue(acc_ref[...] + partial_product(a_refs[-1]), s_ref, act, o_ref.dtype)


def _matmul(a, w, layer, out_dtype, colscale=None, act=None, b_cols=None, tm=2048, tn=1024, tk=2048):
    split_a = isinstance(a, tuple)
    a_list = list(a) if split_a else [a]
    m = a_list[0].shape[0]
    kdim = sum(x.shape[1] for x in a_list)
    c0, n = b_cols if b_cols is not None else (0, w.shape[2])
    jb = c0 // tn
    assert c0 % tn == 0 and n % tn == 0 and m % tm == 0 and kdim % tk == 0
    nk = kdim // tk
    scaled = colscale is not None
    if split_a:
        assert nk == 2 and all(x.shape == (m, tk) for x in a_list)
        in_specs = [pl.BlockSpec((tm, tk), lambda i, j, k: (i, 0), pipeline_mode=pl.Buffered(1))
                    for _ in a_list]
    else:
        in_specs = [pl.BlockSpec((tm, tk), lambda i, j, k: (i, k))]
    in_specs.append(pl.BlockSpec((None, tk, tn), lambda i, j, k: (layer, k, j + jb)))
    args = a_list + [w]
    if scaled:
        in_specs.append(pl.BlockSpec((1, tn), lambda i, j, k: (0, j)))
        args.append(colscale.reshape(1, n).astype(F32))
    scratch = [pltpu.VMEM((tm, tn), F32)] if nk > 1 and out_dtype != F32 else []
    return pl.pallas_call(
        functools.partial(_mm_kernel, nk=nk, act=act, scaled=scaled, split_a=split_a),
        grid=(m // tm, n // tn, nk),
        in_specs=in_specs,
        out_specs=pl.BlockSpec((tm, tn), lambda i, j, k: (i, j), pipeline_mode=pl.Buffered(1)),
        out_shape=jax.ShapeDtypeStruct((m, n), out_dtype),
        scratch_shapes=scratch,
        compiler_params=_cparams(("parallel", "parallel", "arbitrary")),
        name="matmul",
    )(*args)


def _norm_res_kernel(x_ref, y_ref, g_ref, *rest, with_next):
    y = y_ref[...]
    ms = jnp.mean(y * y, axis=-1, keepdims=True)
    xn = x_ref[...] + y * lax.rsqrt(ms + RMS_EPS) * g_ref[...]
    if with_next:
        g2_ref, o_ref, h_ref = rest
        o_ref[...] = xn
        ms2 = jnp.mean(xn * xn, axis=-1, keepdims=True)
        h_ref[...] = (xn * lax.rsqrt(ms2 + RMS_EPS) * g2_ref[...]).astype(h_ref.dtype)
    else:
        o_ref, = rest
        o_ref[...] = xn


def _norm_res(x, y, g, g_next=None, tm=128):
    m, d = x.shape
    with_next = g_next is not None
    row = pl.BlockSpec((tm, d), lambda i: (i, 0))
    vec = pl.BlockSpec((1, d), lambda i: (0, 0))
    in_specs = [row, row, vec]
    args = [x, y, g.reshape(1, d)]
    out_shape = jax.ShapeDtypeStruct((m, d), F32)
    out_specs = row
    if with_next:
        in_specs.append(vec)
        args.append(g_next.reshape(1, d))
        out_shape = (out_shape, jax.ShapeDtypeStruct((m, d), BF16))
        out_specs = (row, row)
    return pl.pallas_call(
        functools.partial(_norm_res_kernel, with_next=with_next),
        grid=(m // tm,),
        in_specs=in_specs,
        out_specs=out_specs,
        out_shape=out_shape,
        compiler_params=_cparams(("parallel",)),
        name="norm_res",
    )(*args)


def _t5_bucket_np(rel):
    half = NUM_BUCKETS // 2
    max_exact = half // 2
    ret = np.where(rel > 0, half, 0)
    n = np.abs(rel)
    nf = np.maximum(n, 1).astype(np.float32)
    large = max_exact + (np.log(nf / np.float32(max_exact)) / np.float32(math.log(MAX_DISTANCE / max_exact))
                         * np.float32(half - max_exact)).astype(np.int32)
    large = np.minimum(large, half - 1)
    return (ret + np.where(n < max_exact, n, large)).astype(np.int32)


def _bias_kernel(tab_ref, idx_ref, o_ref, *, col0, scale, rb, present):
    h = pl.program_id(0)
    for i, buckets in enumerate(present):
        idx = idx_ref[i * rb:(i + 1) * rb, :]
        acc = jnp.full(idx.shape, NEG_INF, F32) if -1 in buckets else None
        for b in buckets:
            if b < 0:
                continue
            val = tab_ref[b, col0 + h] * scale
            acc = jnp.full(idx.shape, val, F32) if acc is None else jnp.where(idx == b, val, acc)
        o_ref[0, i * rb:(i + 1) * rb, :] = acc


def _build_bias(rel_bias, idx, n_heads, col0, rb=128, scale=1.0):
    r, c = idx.shape
    present = tuple(tuple(int(b) for b in np.unique(idx[i:i + rb])) for i in range(0, r, rb))
    return pl.pallas_call(
        functools.partial(_bias_kernel, col0=col0, scale=scale, rb=rb, present=present),
        grid=(n_heads,),
        in_specs=[pl.BlockSpec(memory_space=pltpu.SMEM),
                  pl.BlockSpec((r, c), lambda h: (0, 0))],
        out_specs=pl.BlockSpec((1, r, c), lambda h: (h, 0, 0)),
        out_shape=jax.ShapeDtypeStruct((n_heads, r, c), F32),
        compiler_params=_cparams(("parallel",)),
        name="bias_tiles",
    )(rel_bias, jnp.asarray(idx))


def _dil_bias_idx():
    row = np.arange(DIL_QBLK)[:, None]
    col = np.arange(DIL_KWIN)[None, :]
    tiles = []
    for d in DILATIONS:
        for off in (0, -DIL_RADIUS, -2 * DIL_RADIUS):
            rel = off + col - row
            idx = _t5_bucket_np(rel * d)
            tiles.append(np.where(np.abs(rel) <= DIL_RADIUS, idx, -1))
    return np.concatenate(tiles, axis=0).astype(np.int32)


def _diff_bias_idx(seq):
    far = np.arange(DIFF_FAR, seq)
    assert np.all(_t5_bucket_np(-far) == NUM_BUCKETS // 2 - 1) and np.all(_t5_bucket_np(far) == NUM_BUCKETS - 1)
    row = np.arange(DIFF_BAND + 2 * DIFF_FAR)[:, None]
    col = np.arange(DIFF_QBLK)[None, :]
    return _t5_bucket_np(row - 2 * DIFF_FAR - col)


def _diff_key_steps(seq):
    key = np.arange(seq)[:, None]
    lane = np.arange(LANES)[None, :]
    return np.where((lane < 64) & (key >= LANES * (lane & 31)), 1.0, 0.0).astype(np.float32)


def _dil_kernel(q_ref, k_ref, v_ref, bias_ref, o_ref, m_scr, l_scr, n_scr, *, seq):
    n_it = seq // DIL_QBLK
    for bi, d in enumerate(DILATIONS):
        sub_len = seq // d
        nblk = sub_len // DIL_QBLK
        log_d = d.bit_length() - 1
        first = bi == 0
        last = bi == len(DILATIONS) - 1

        def rows(start, size, d=d):
            if d == 1:
                return pl.ds(pl.multiple_of(start, DIL_QBLK // 2), size)
            return pl.ds(start, size, stride=d)

        def body(grp, carry, d=d, nblk=nblk, log_d=log_d, first=first, last=last, bi=bi,
                 sub_len=sub_len, rows=rows):
            qsls, scores, vbs = [], [], []
            for u in range(DIL_UNROLL):
                it = grp * DIL_UNROLL + u
                r = it & (d - 1)
                j = it >> log_d
                q0 = j * DIL_QBLK
                ws = jnp.clip(q0 - DIL_RADIUS, 0, sub_len - DIL_KWIN)
                variant = jnp.where(j == 0, 0, jnp.where(j == nblk - 1, 2, 1))
                qsl = rows(r + d * q0, DIL_QBLK)
                ksl = rows(r + d * ws, DIL_KWIN)
                qb = q_ref[qsl, :].astype(BF16)
                kb = k_ref[ksl, :].astype(BF16)
                s = lax.dot_general(qb, kb, (((1,), (1,)), ((), ())), preferred_element_type=F32)
                qsls.append(qsl)
                scores.append(s + bias_ref[0, bi * 3 + variant])
                vbs.append(v_ref[ksl, :].astype(BF16))
            stats = []
            for s in scores:
                m = jnp.max(s, axis=1, keepdims=True)
                p = jnp.exp(s - m)
                stats.append((m, jnp.sum(p, axis=1, keepdims=True), p.astype(BF16)))
            nums = [jnp.dot(p, vb, preferred_element_type=F32) for (_, _, p), vb in zip(stats, vbs)]
            shape = (DIL_QBLK, LANES)
            for qsl, (m, l, _), n in zip(qsls, stats, nums):
                m = jnp.broadcast_to(m, shape)
                l = jnp.broadcast_to(l, shape)
                if not first:
                    m_old = m_scr[qsl, :]
                    m_new = jnp.maximum(m_old, m)
                    a = jnp.exp(m_old - m_new)
                    b = jnp.exp(m - m_new)
                    l = a * l_scr[qsl, :] + b * l
                    n = a * n_scr[qsl, :] + b * n
                    m = m_new
                if last:
                    o_ref[qsl, :] = (n / l).astype(o_ref.dtype)
                else:
                    m_scr[qsl, :] = m
                    l_scr[qsl, :] = l
                    n_scr[qsl, :] = n
            return carry

        lax.fori_loop(0, n_it // DIL_UNROLL, body, 0)


def _dil_attention(proj, bias_tiles, seq):
    bsz = proj.shape[0]
    nh = N_HEADS_DIL

    def col(off):
        return pl.BlockSpec((None, seq, HEAD_DIM), lambda b, h: (b, 0, off + h))

    state = pltpu.VMEM((seq, LANES), F32)
    return pl.pallas_call(
        functools.partial(_dil_kernel, seq=seq),
        grid=(bsz, nh),
        in_specs=[col(0), col(nh), col(2 * nh),
                  pl.BlockSpec((1, 3 * len(DILATIONS), DIL_QBLK, DIL_KWIN), lambda b, h: (h, 0, 0, 0))],
        out_specs=pl.BlockSpec((None, seq, HEAD_DIM), lambda b, h: (b, 0, h)),
        out_shape=jax.ShapeDtypeStruct((bsz, seq, DIL_WIDTH), BF16),
        scratch_shapes=[state, state, state],
        compiler_params=_cparams(("parallel", "parallel")),
        name="dilated_attention",
    )(proj, proj, proj, bias_tiles)


def _diff_kernel(tab_ref, lam_ref, q_ref, k_ref, ksteps_ref, vt_ref, t_ref, g_ref, o_ref, o_scr,
                 *, seq, lambda_init):
    h = pl.program_id(0)
    q0 = pl.program_id(2) * DIFF_QBLK
    ks = jnp.clip(q0 - DIFF_FAR, 0, seq - DIFF_BAND)
    ts = ks - q0 + 2 * DIFF_FAR
    step_lo = lax.shift_right_logical(ks, 7)
    step_hi = step_lo + DIFF_BAND // LANES
    n_piece = seq // LANES

    lam = (jnp.exp(jnp.sum(lam_ref[0:1, :] * lam_ref[1:2, :], axis=1, keepdims=True))
           - jnp.exp(jnp.sum(lam_ref[2:3, :] * lam_ref[3:4, :], axis=1, keepdims=True)) + lambda_init)

    c_left = tab_ref[NUM_BUCKETS // 2 - 1, N_HEADS_DIL + h] * LOG2E
    c_right = tab_ref[NUM_BUCKETS - 1, N_HEADS_DIL + h] * LOG2E
    lane = lax.broadcasted_iota(jnp.int32, (1, LANES), 1)
    lj = lane & 31
    c = (jnp.where(lj == 0, c_left, 0.0) + jnp.where(lj == step_lo, -c_left, 0.0)
         + jnp.where(lj == step_hi, c_right, 0.0))
    c = jnp.where(lane < 64, c, 0.0)
    c_hi = c.astype(BF16).astype(F32)
    q_extra = jnp.where(lane < 32, c_hi, c - c_hi).astype(BF16)
    q_extra = jnp.broadcast_to(q_extra, (DIFF_QBLK, LANES))

    q = q_ref[...]
    qlane = lax.broadcasted_iota(jnp.int32, q.shape, 1)
    zero = jnp.zeros_like(q)
    q_both = jnp.concatenate(
        [jnp.concatenate([jnp.where(qlane < DIFF_QK_DIM, q, zero), q_extra], axis=1),
         jnp.concatenate([jnp.where(qlane >= DIFF_QK_DIM, q, zero), q_extra], axis=1)], axis=0)

    pieces_per_chunk = DIFF_CHUNK // LANES

    def piece_start(g):
        return pl.multiple_of(((step_lo + g) & (n_piece - 1)) * LANES, LANES)

    def scores(g0):
        k_c = jnp.concatenate(
            [jnp.concatenate([k_ref[pl.ds(piece_start(g0 + i), LANES), :],
                              ksteps_ref[pl.ds(piece_start(g0 + i), LANES), :]], axis=1)
             for i in range(pieces_per_chunk)], axis=0)
        return lax.dot_general(k_c, q_both, (((1,), (1,)), ((), ())), preferred_element_type=F32)

    def band_bias(g0):
        tile = t_ref[0, pl.ds(pl.multiple_of(ts + g0 * LANES, LANES), DIFF_CHUNK), :]
        return jnp.concatenate([tile, tile], axis=1)

    def values_t(g0):
        return jnp.concatenate([vt_ref[:, pl.ds(piece_start(g0 + i), LANES)]
                                for i in range(pieces_per_chunk)], axis=1)

    def finish(o, l):
        on = o / l
        od = on[:, :DIFF_QBLK] - lam * on[:, DIFF_QBLK:]
        ms = jnp.mean(od * od, axis=0, keepdims=True)
        y = od * lax.rsqrt(ms + SUBLN_EPS) * g_ref[...] * (1.0 - lambda_init)
        o_ref[...] = y.T.astype(o_ref.dtype)

    chunk_starts = list(range(0, n_piece, pieces_per_chunk))
    width = 2 * DIFF_QBLK
    s = scores(0)
    ref = jnp.max(s + band_bias(0), axis=0, keepdims=True)
    l8 = jnp.zeros((8, width), F32)
    o = None
    for ci, g0 in enumerate(chunk_starts):
        if ci + 1 < len(chunk_starts):
            s_next = scores(chunk_starts[ci + 1])
        if g0 * LANES < DIFF_BAND:
            s = s + band_bias(g0)
        p = jnp.exp2(s - ref)
        l8 = l8 + jnp.sum(p.reshape(DIFF_CHUNK // 8, 8, width), axis=0)
        o_c = jnp.dot(values_t(g0), p.astype(BF16), preferred_element_type=F32)
        o = o_c if o is None else o + o_c
        s = s_next
    l = jnp.sum(l8, axis=0, keepdims=True)
    finish(o, l)

    def all_finite(x):
        return jnp.min(jnp.where(jnp.isfinite(x), 1.0, 0.0))

    @pl.when(jnp.logical_or(jnp.minimum(all_finite(o), all_finite(l)) < 0.5, jnp.max(l) > DIFF_MAX_DENOM))
    def _():
        def chunk(c, carry, in_band):
            m_old, l_old = carry
            g0 = c * pieces_per_chunk
            sc = scores(g0)
            if in_band:
                sc = sc + band_bias(g0)
            m_new = jnp.maximum(m_old, jnp.max(sc, axis=0, keepdims=True))
            alpha = jnp.exp2(m_old - m_new)
            pc = jnp.exp2(sc - m_new)
            o_scr[...] = alpha * o_scr[...] + jnp.dot(values_t(g0), pc.astype(BF16), preferred_element_type=F32)
            return m_new, alpha * l_old + jnp.sum(pc, axis=0, keepdims=True)

        o_scr[...] = jnp.zeros_like(o_scr)
        carry = (jnp.full((1, 2 * DIFF_QBLK), NEG_INF, F32), jnp.zeros((1, 2 * DIFF_QBLK), F32))
        n_band = DIFF_BAND // DIFF_CHUNK
        carry = lax.fori_loop(0, n_band, functools.partial(chunk, in_band=True), carry)
        _, l_slow = lax.fori_loop(n_band, len(chunk_starts), functools.partial(chunk, in_band=False), carry)
        finish(o_scr[...], l_slow)


def _diff_attention(rel_bias, proj, vt, t_bias, lam_vecs, g_subln, seq, lambda_init):
    bsz = proj.shape[0]
    nh = N_HEADS_DIFF
    nq = seq // DIFF_QBLK
    assert seq == 32 * LANES and DIFF_BAND % DIFF_CHUNK == 0 and seq % DIFF_CHUNK == 0
    ksteps = jnp.asarray(_diff_key_steps(seq), BF16)
    return pl.pallas_call(
        functools.partial(_diff_kernel, seq=seq, lambda_init=lambda_init),
        grid=(nh, bsz, nq),
        in_specs=[pl.BlockSpec(memory_space=pltpu.SMEM),
                  pl.BlockSpec((4, DIFF_QK_DIM), lambda h, b, i: (0, 0)),
                  pl.BlockSpec((None, DIFF_QBLK, HEAD_DIM), lambda h, b, i: (b, i, h)),
                  pl.BlockSpec((None, seq, HEAD_DIM), lambda h, b, i: (b, 0, nh + h)),
                  pl.BlockSpec((seq, LANES), lambda h, b, i: (0, 0)),
                  pl.BlockSpec((None, None, HEAD_DIM, seq), lambda h, b, i: (b, h, 0, 0)),
                  pl.BlockSpec((1, DIFF_BAND + 2 * DIFF_FAR, DIFF_QBLK), lambda h, b, i: (h, 0, 0)),
                  pl.BlockSpec((HEAD_DIM, 1), lambda h, b, i: (0, 0))],
        out_specs=pl.BlockSpec((None, DIFF_QBLK, HEAD_DIM), lambda h, b, i: (b, i, h)),
        out_shape=jax.ShapeDtypeStruct((bsz, seq, DIFF_WIDTH), BF16),
        scratch_shapes=[pltpu.VMEM((HEAD_DIM, 2 * DIFF_QBLK), F32)],
        compiler_params=_cparams(("parallel", "parallel", "parallel")),
        name="diff_attention",
    )(rel_bias, lam_vecs, proj, proj, ksteps, vt, t_bias, g_subln.reshape(HEAD_DIM, 1))


def kernel(x, rel_bias, g_pre_mix, w_in, lambda_q1, lambda_k1, lambda_q2, lambda_k2, g_subln, w_out,
           g_post_mix, g_pre_mlp, w_up, w_down, g_post_mlp):
    bsz, seq, d = x.shape
    m = bsz * seq
    xf = x.reshape(m, d)

    dil_tiles = _build_bias(rel_bias, _dil_bias_idx(), N_HEADS_DIL, 0)
    dil_tiles = dil_tiles.reshape(N_HEADS_DIL, 3 * len(DILATIONS), DIL_QBLK, DIL_KWIN)
    diff_t = _build_bias(rel_bias, _diff_bias_idx(seq), N_HEADS_DIFF, N_HEADS_DIL, scale=LOG2E)

    ones = jnp.ones((DIL_WIDTH,), F32)
    scale_dil = jnp.concatenate([ones * (HEAD_DIM ** -0.5), ones, ones])
    scale_diff = jnp.concatenate([ones * (DIFF_QK_DIM ** -0.5 * LOG2E), ones, ones])

    h = _rmsnorm(xf, g_pre_mix[0])
    for l in range(DEPTH):
        lambda_init = 0.8 - 0.6 * math.exp(-0.3 * l)
        proj_a = _matmul(h, w_in, l, F32, colscale=scale_dil, b_cols=(0, 3 * DIL_WIDTH))
        proj_d = _matmul(h, w_in, l, BF16, colscale=scale_diff, b_cols=(3 * DIL_WIDTH, 3 * DIFF_WIDTH))
        out_a = _dil_attention(proj_a.reshape(bsz, seq, 3 * DIL_WIDTH), dil_tiles, seq)
        lam_vecs = jnp.stack([lambda_q1[l], lambda_k1[l], lambda_q2[l], lambda_k2[l]]).astype(F32)
        proj_d = proj_d.reshape(bsz, seq, 3 * DIFF_WIDTH)
        vt = proj_d[:, :, 2 * DIFF_WIDTH:].reshape(bsz, seq, N_HEADS_DIFF, HEAD_DIM).transpose(0, 2, 3, 1)
        out_d = _diff_attention(rel_bias, proj_d, vt, diff_t, lam_vecs, g_subln[l], seq, lambda_init)
        y = _matmul((out_a.reshape(m, DIL_WIDTH), out_d.reshape(m, DIFF_WIDTH)), w_out, l, F32)
        xf, h = _norm_res(xf, y, g_post_mix[l], g_pre_mlp[l])
        u = _matmul(h, w_up, l, BF16, act="relu2")
        y = _matmul(u, w_down, l, F32)
        if l + 1 < DEPTH:
            xf, h = _norm_res(xf, y, g_post_mlp[l], g_pre_mix[l + 1])
        else:
            xf = _norm_res(xf, y, g_post_mlp[l])
    return xf.reshape(bsz, seq, d)
```

```python
import functools
import math

import numpy as np
import jax
import jax.numpy as jnp
from jax import lax
from jax.experimental import pallas as pl
from jax.experimental.pallas import tpu as pltpu

D_MODEL = 4096
DEPTH = 2
HEAD_DIM = 128
N_HEADS_DIL = 16
N_HEADS_DIFF = 16
DIFF_QK_DIM = 64
DIL_WIDTH = N_HEADS_DIL * HEAD_DIM
DIFF_WIDTH = N_HEADS_DIFF * HEAD_DIM
DILATIONS = (16, 4, 1)
DIL_RADIUS = 64
NUM_BUCKETS = 32
MAX_DISTANCE = 1024
RMS_EPS = 1e-6
SUBLN_EPS = 1e-5
NEG_INF = -1e30

LANES = 128
DIL_QBLK = 128
DIL_KWIN = 256
DIL_UNROLL = 8
DIFF_QBLK = 512
DIFF_SUBS = 2
DIFF_FAR = 768
DIFF_BAND = 2 * DIFF_FAR + DIFF_QBLK
DIFF_CHUNK = 512
DIFF_MAX_DENOM = 2.0 ** 80
LOG2E = 1.4426950408889634
VMEM_LIMIT = 56 * 1024 * 1024

F32 = jnp.float32
BF16 = jnp.bfloat16


def _cparams(sem, flags=None):
    return pltpu.CompilerParams(dimension_semantics=sem, vmem_limit_bytes=VMEM_LIMIT, flags=flags)


def _rmsnorm_kernel(x_ref, g_ref, o_ref):
    x = x_ref[...]
    ms = jnp.mean(x * x, axis=-1, keepdims=True)
    o_ref[...] = (x * lax.rsqrt(ms + RMS_EPS) * g_ref[...]).astype(o_ref.dtype)


def _rmsnorm(x, g, tm=256):
    m, d = x.shape
    return pl.pallas_call(
        _rmsnorm_kernel,
        grid=(m // tm,),
        in_specs=[pl.BlockSpec((tm, d), lambda i: (i, 0)),
                  pl.BlockSpec((1, d), lambda i: (0, 0))],
        out_specs=pl.BlockSpec((tm, d), lambda i: (i, 0)),
        out_shape=jax.ShapeDtypeStruct((m, d), BF16),
        compiler_params=_cparams(("parallel",)),
        name="rmsnorm",
    )(x, g.reshape(1, d))


def _mm_epilogue(r, s_ref, act, dtype):
    if s_ref is not None:
        r = r * s_ref[...]
    if act == "relu2":
        r = jnp.square(jnp.maximum(r, 0.0))
    return r.astype(dtype)


def _mm_kernel(*refs, nk, act, scaled, split_a):
    n_a = 2 if split_a else 1
    a_refs = refs[:n_a]
    b_ref = refs[n_a]
    s_ref = refs[n_a + 1] if scaled else None
    o_ref = refs[n_a + 1 + scaled]
    rest = refs[n_a + 2 + scaled:]

    def partial_product(a_ref):
        return jnp.dot(a_ref[...], b_ref[...].astype(BF16), preferred_element_type=F32)

    if nk == 1:
        o_ref[...] = _mm_epilogue(partial_product(a_refs[0]), s_ref, act, o_ref.dtype)
        return
    acc_ref, = rest
    k = pl.program_id(2)

    @pl.when(k == 0)
    def _():
        acc_ref[...] = partial_product(a_refs[0])

    @pl.when(jnp.logical_and(k > 0, k < nk - 1))
    def _():
        acc_ref[...] += partial_product(a_refs[0])

    @pl.when(k == nk - 1)
    def _():
        o_ref[...] = _mm_epilogue(acc_ref[...] + partial_product(a_refs[-1]), s_ref, act, o_ref.dtype)


def _matmul(a, w, layer, out_dtype, colscale=None, act=None, b_cols=None, tm=2048, tn=1024, tk=1024):
    split_a = isinstance(a, tuple)
    a_list = list(a) if split_a else [a]
    m = a_list[0].shape[0]
    kdim = sum(x.shape[1] for x in a_list)
    c0, n = b_cols if b_cols is not None else (0, w.shape[2])
    jb = c0 // tn
    assert c0 % tn == 0 and n % tn == 0 and m % tm == 0 and kdim % tk == 0
    nk = kdim // tk
    scaled = colscale is not None
    if split_a:
        assert nk == 2 and all(x.shape == (m, tk) for x in a_list)
        in_specs = [pl.BlockSpec((tm, tk), lambda i, j, k: (i, 0), pipeline_mode=pl.Buffered(1))
                    for _ in a_list]
    else:
        in_specs = [pl.BlockSpec((tm, tk), lambda i, j, k: (i, k))]
    in_specs.append(pl.BlockSpec((None, tk, tn), lambda i, j, k: (layer, k, j + jb)))
    args = a_list + [w]
    if scaled:
        in_specs.append(pl.BlockSpec((1, tn), lambda i, j, k: (0, j)))
        args.append(colscale.reshape(1, n).astype(F32))
    scratch = [pltpu.VMEM((tm, tn), F32)] if nk > 1 else []
    return pl.pallas_call(
        functools.partial(_mm_kernel, nk=nk, act=act, scaled=scaled, split_a=split_a),
        grid=(m // tm, n // tn, nk),
        in_specs=in_specs,
        out_specs=pl.BlockSpec((tm, tn), lambda i, j, k: (i, j)),
        out_shape=jax.ShapeDtypeStruct((m, n), out_dtype),
        scratch_shapes=scratch,
        compiler_params=_cparams(("parallel", "parallel", "arbitrary")),
        name="matmul",
    )(*args)


def _norm_res_kernel(x_ref, y_ref, g_ref, *rest, with_next):
    y = y_ref[...]
    ms = jnp.mean(y * y, axis=-1, keepdims=True)
    xn = x_ref[...] + y * lax.rsqrt(ms + RMS_EPS) * g_ref[...]
    if with_next:
        g2_ref, o_ref, h_ref = rest
        o_ref[...] = xn
        ms2 = jnp.mean(xn * xn, axis=-1, keepdims=True)
        h_ref[...] = (xn * lax.rsqrt(ms2 + RMS_EPS) * g2_ref[...]).astype(h_ref.dtype)
    else:
        o_ref, = rest
        o_ref[...] = xn


def _norm_res(x, y, g, g_next=None, tm=128):
    m, d = x.shape
    with_next = g_next is not None
    row = pl.BlockSpec((tm, d), lambda i: (i, 0))
    vec = pl.BlockSpec((1, d), lambda i: (0, 0))
    in_specs = [row, row, vec]
    args = [x, y, g.reshape(1, d)]
    out_shape = jax.ShapeDtypeStruct((m, d), F32)
    out_specs = row
    if with_next:
        in_specs.append(vec)
        args.append(g_next.reshape(1, d))
        out_shape = (out_shape, jax.ShapeDtypeStruct((m, d), BF16))
        out_specs = (row, row)
    return pl.pallas_call(
        functools.partial(_norm_res_kernel, with_next=with_next),
        grid=(m // tm,),
        in_specs=in_specs,
        out_specs=out_specs,
        out_shape=out_shape,
        compiler_params=_cparams(("parallel",)),
        name="norm_res",
    )(*args)


def _t5_bucket_np(rel):
    half = NUM_BUCKETS // 2
    max_exact = half // 2
    ret = np.where(rel > 0, half, 0)
    n = np.abs(rel)
    nf = np.maximum(n, 1).astype(np.float32)
    large = max_exact + (np.log(nf / np.float32(max_exact)) / np.float32(math.log(MAX_DISTANCE / max_exact))
                         * np.float32(half - max_exact)).astype(np.int32)
    large = np.minimum(large, half - 1)
    return (ret + np.where(n < max_exact, n, large)).astype(np.int32)


def _bias_kernel(tab_ref, idx_ref, o_ref, *, col0, scale, rb, present):
    h = pl.program_id(0)
    for i, buckets in enumerate(present):
        idx = idx_ref[i * rb:(i + 1) * rb, :]
        acc = jnp.full(idx.shape, NEG_INF, F32) if -1 in buckets else None
        for b in buckets:
            if b < 0:
                continue
            val = tab_ref[b, col0 + h] * scale
            acc = jnp.full(idx.shape, val, F32) if acc is None else jnp.where(idx == b, val, acc)
        o_ref[0, i * rb:(i + 1) * rb, :] = acc


def _build_bias(rel_bias, idx, n_heads, col0, rb=128, scale=1.0):
    r, c = idx.shape
    present = tuple(tuple(int(b) for b in np.unique(idx[i:i + rb])) for i in range(0, r, rb))
    return pl.pallas_call(
        functools.partial(_bias_kernel, col0=col0, scale=scale, rb=rb, present=present),
        grid=(n_heads,),
        in_specs=[pl.BlockSpec(memory_space=pltpu.SMEM),
                  pl.BlockSpec((r, c), lambda h: (0, 0))],
        out_specs=pl.BlockSpec((1, r, c), lambda h: (h, 0, 0)),
        out_shape=jax.ShapeDtypeStruct((n_heads, r, c), F32),
        compiler_params=_cparams(("parallel",)),
        name="bias_tiles",
    )(rel_bias, jnp.asarray(idx))


def _dil_bias_idx():
    row = np.arange(DIL_QBLK)[:, None]
    col = np.arange(DIL_KWIN)[None, :]
    tiles = []
    for d in DILATIONS:
        for off in (0, -DIL_RADIUS, -2 * DIL_RADIUS):
            rel = off + col - row
            idx = _t5_bucket_np(rel * d)
            tiles.append(np.where(np.abs(rel) <= DIL_RADIUS, idx, -1))
    return np.concatenate(tiles, axis=0).astype(np.int32)


def _diff_bias_idx(seq):
    far = np.arange(DIFF_FAR, seq)
    assert np.all(_t5_bucket_np(-far) == NUM_BUCKETS // 2 - 1) and np.all(_t5_bucket_np(far) == NUM_BUCKETS - 1)
    row = np.arange(DIFF_BAND + 2 * DIFF_FAR)[:, None]
    col = np.arange(DIFF_QBLK)[None, :]
    return _t5_bucket_np(row - 2 * DIFF_FAR - col)


def _diff_key_steps(seq):
    key = np.arange(seq)[:, None]
    lane = np.arange(LANES)[None, :]
    return np.where((lane < 64) & (key >= LANES * (lane & 31)), 1.0, 0.0).astype(np.float32)


def _dil_kernel(q_ref, k_ref, v_ref, bias_ref, o_ref, m_scr, l_scr, n_scr, *, seq):
    n_it = seq // DIL_QBLK
    for bi, d in enumerate(DILATIONS):
        sub_len = seq // d
        nblk = sub_len // DIL_QBLK
        log_d = d.bit_length() - 1
        first = bi == 0
        last = bi == len(DILATIONS) - 1

        def rows(start, size, d=d):
            if d == 1:
                return pl.ds(pl.multiple_of(start, DIL_QBLK // 2), size)
            return pl.ds(start, size, stride=d)

        def body(grp, carry, d=d, nblk=nblk, log_d=log_d, first=first, last=last, bi=bi,
                 sub_len=sub_len, rows=rows):
            qsls, scores, vbs = [], [], []
            for u in range(DIL_UNROLL):
                it = grp * DIL_UNROLL + u
                r = it & (d - 1)
                j = it >> log_d
                q0 = j * DIL_QBLK
                ws = jnp.clip(q0 - DIL_RADIUS, 0, sub_len - DIL_KWIN)
                variant = jnp.where(j == 0, 0, jnp.where(j == nblk - 1, 2, 1))
                qsl = rows(r + d * q0, DIL_QBLK)
                ksl = rows(r + d * ws, DIL_KWIN)
                qb = q_ref[qsl, :].astype(BF16)
                kb = k_ref[ksl, :].astype(BF16)
                s = lax.dot_general(qb, kb, (((1,), (1,)), ((), ())), preferred_element_type=F32)
                qsls.append(qsl)
                scores.append(s + bias_ref[0, bi * 3 + variant])
                vbs.append(v_ref[ksl, :].astype(BF16))
            stats = []
            for s in scores:
                m = jnp.max(s, axis=1, keepdims=True)
                p = jnp.exp(s - m)
                stats.append((m, jnp.sum(p, axis=1, keepdims=True), p.astype(BF16)))
            nums = [jnp.dot(p, vb, preferred_element_type=F32) for (_, _, p), vb in zip(stats, vbs)]
            shape = (DIL_QBLK, LANES)
            for qsl, (m, l, _), n in zip(qsls, stats, nums):
                m = jnp.broadcast_to(m, shape)
                l = jnp.broadcast_to(l, shape)
                if not first:
                    m_old = m_scr[qsl, :]
                    m_new = jnp.maximum(m_old, m)
                    a = jnp.exp(m_old - m_new)
                    b = jnp.exp(m - m_new)
                    l = a * l_scr[qsl, :] + b * l
                    n = a * n_scr[qsl, :] + b * n
                    m = m_new
                if last:
                    o_ref[qsl, :] = (n / l).astype(o_ref.dtype)
                else:
                    m_scr[qsl, :] = m
                    l_scr[qsl, :] = l
                    n_scr[qsl, :] = n
            return carry

        lax.fori_loop(0, n_it // DIL_UNROLL, body, 0)


def _dil_attention(proj, bias_tiles, seq):
    bsz = proj.shape[0]
    nh = N_HEADS_DIL

    def col(off):
        return pl.BlockSpec((None, seq, HEAD_DIM), lambda b, h: (b, 0, off + h))

    state = pltpu.VMEM((seq, LANES), F32)
    return pl.pallas_call(
        functools.partial(_dil_kernel, seq=seq),
        grid=(bsz, nh),
        in_specs=[col(0), col(nh), col(2 * nh),
                  pl.BlockSpec((1, 3 * len(DILATIONS), DIL_QBLK, DIL_KWIN), lambda b, h: (h, 0, 0, 0))],
        out_specs=pl.BlockSpec((None, seq, HEAD_DIM), lambda b, h: (b, 0, h)),
        out_shape=jax.ShapeDtypeStruct((bsz, seq, DIL_WIDTH), BF16),
        scratch_shapes=[state, state, state],
        compiler_params=_cparams(("parallel", "parallel")),
        name="dilated_attention",
    )(proj, proj, proj, bias_tiles)


def _diff_kernel(tab_ref, lam_ref, q_ref, k_ref, ksteps_ref, vt_ref, t_ref, g_ref, o_ref, o_scr,
                 *, seq, lambda_init):
    h = pl.program_id(0)
    n_piece = seq // LANES
    pieces_per_chunk = DIFF_CHUNK // LANES
    chunk_starts = list(range(0, n_piece, pieces_per_chunk))
    width = 2 * DIFF_QBLK

    lam = (jnp.exp(jnp.sum(lam_ref[0:1, :] * lam_ref[1:2, :], axis=1, keepdims=True))
           - jnp.exp(jnp.sum(lam_ref[2:3, :] * lam_ref[3:4, :], axis=1, keepdims=True)) + lambda_init)
    c_left = tab_ref[NUM_BUCKETS // 2 - 1, N_HEADS_DIL + h] * LOG2E
    c_right = tab_ref[NUM_BUCKETS - 1, N_HEADS_DIL + h] * LOG2E
    lane = lax.broadcasted_iota(jnp.int32, (1, LANES), 1)
    lj = lane & 31
    qlane = lax.broadcasted_iota(jnp.int32, (DIFF_QBLK, LANES), 1)

    class SubBlock:
        def __init__(self, sb):
            rows = slice(sb * DIFF_QBLK, (sb + 1) * DIFF_QBLK)
            q0 = (pl.program_id(2) * DIFF_SUBS + sb) * DIFF_QBLK
            ks = jnp.clip(q0 - DIFF_FAR, 0, seq - DIFF_BAND)
            ts = ks - q0 + 2 * DIFF_FAR
            step_lo = lax.shift_right_logical(ks, 7)
            step_hi = step_lo + DIFF_BAND // LANES
            c = (jnp.where(lj == 0, c_left, 0.0) + jnp.where(lj == step_lo, -c_left, 0.0)
                 + jnp.where(lj == step_hi, c_right, 0.0))
            c = jnp.where(lane < 64, c, 0.0)
            c_hi = c.astype(BF16).astype(F32)
            q_extra = jnp.where(lane < 32, c_hi, c - c_hi).astype(BF16)
            q_extra = jnp.broadcast_to(q_extra, (DIFF_QBLK, LANES))
            q = q_ref[rows, :]
            zero = jnp.zeros_like(q)
            q_both = jnp.concatenate(
                [jnp.concatenate([jnp.where(qlane < DIFF_QK_DIM, q, zero), q_extra], axis=1),
                 jnp.concatenate([jnp.where(qlane >= DIFF_QK_DIM, q, zero), q_extra], axis=1)], axis=0)

            def piece_start(g):
                return pl.multiple_of(((step_lo + g) & (n_piece - 1)) * LANES, LANES)

            def scores(g0):
                k_c = jnp.concatenate(
                    [jnp.concatenate([k_ref[pl.ds(piece_start(g0 + i), LANES), :],
                                      ksteps_ref[pl.ds(piece_start(g0 + i), LANES), :]], axis=1)
                     for i in range(pieces_per_chunk)], axis=0)
                return lax.dot_general(k_c, q_both, (((1,), (1,)), ((), ())), preferred_element_type=F32)

            def band_bias(g0, rows_n=DIFF_CHUNK):
                tile = t_ref[0, pl.ds(pl.multiple_of(ts + g0 * LANES, LANES), rows_n), :]
                return jnp.concatenate([tile, tile], axis=1)

            def values_t(g0):
                return jnp.concatenate([vt_ref[:, pl.ds(piece_start(g0 + i), LANES)]
                                        for i in range(pieces_per_chunk)], axis=1)

            def finish(o, l):
                on = o / l
                od = on[:, :DIFF_QBLK] - lam * on[:, DIFF_QBLK:]
                ms = jnp.mean(od * od, axis=0, keepdims=True)
                y = od * lax.rsqrt(ms + SUBLN_EPS) * g_ref[...] * (1.0 - lambda_init)
                o_ref[rows, :] = y.T.astype(o_ref.dtype)

            self.scores, self.band_bias, self.values_t, self.finish = scores, band_bias, values_t, finish

    subs = [SubBlock(sb) for sb in range(DIFF_SUBS)]
    results = []
    s_next = subs[0].scores(0)
    for sb, sub in enumerate(subs):
        s = s_next
        ref = jnp.max(s[:LANES] + sub.band_bias(0, LANES), axis=0, keepdims=True)
        l8 = jnp.zeros((8, width), F32)
        o = None
        for ci, g0 in enumerate(chunk_starts):
            if ci + 1 < len(chunk_starts):
                s_next = sub.scores(chunk_starts[ci + 1])
            elif sb + 1 < len(subs):
                s_next = subs[sb + 1].scores(0)
            if g0 * LANES < DIFF_BAND:
                s = s + sub.band_bias(g0)
            p = jnp.exp2(s - ref)
            l8 = l8 + jnp.sum(p.reshape(DIFF_CHUNK // 8, 8, width), axis=0)
            o_c = jnp.dot(sub.values_t(g0), p.astype(BF16), preferred_element_type=F32)
            o = o_c if o is None else o + o_c
            s = s_next
        l = jnp.sum(l8, axis=0, keepdims=True)
        sub.finish(o, l)
        results.append((o, l))

    def all_finite(x):
        return jnp.min(jnp.where(jnp.isfinite(x), 1.0, 0.0))

    for sub, (o, l) in zip(subs, results):
        @pl.when(jnp.logical_or(jnp.minimum(all_finite(o), all_finite(l)) < 0.5, jnp.max(l) > DIFF_MAX_DENOM))
        def _(sub=sub):
            def chunk(c, carry, in_band):
                m_old, l_old = carry
                g0 = c * pieces_per_chunk
                sc = sub.scores(g0)
                if in_band:
                    sc = sc + sub.band_bias(g0)
                m_new = jnp.maximum(m_old, jnp.max(sc, axis=0, keepdims=True))
                alpha = jnp.exp2(m_old - m_new)
                pc = jnp.exp2(sc - m_new)
                o_scr[...] = alpha * o_scr[...] + jnp.dot(sub.values_t(g0), pc.astype(BF16),
                                                          preferred_element_type=F32)
                return m_new, alpha * l_old + jnp.sum(pc, axis=0, keepdims=True)

            o_scr[...] = jnp.zeros_like(o_scr)
            carry = (jnp.full((1, width), NEG_INF, F32), jnp.zeros((1, width), F32))
            n_band = DIFF_BAND // DIFF_CHUNK
            carry = lax.fori_loop(0, n_band, functools.partial(chunk, in_band=True), carry)
            _, l_slow = lax.fori_loop(n_band, len(chunk_starts), functools.partial(chunk, in_band=False), carry)
            sub.finish(o_scr[...], l_slow)


def _diff_attention(rel_bias, proj, vt, t_bias, lam_vecs, g_subln, seq, lambda_init):
    bsz = proj.shape[0]
    nh = N_HEADS_DIFF
    tq = DIFF_SUBS * DIFF_QBLK
    nq = seq // tq
    assert seq == 32 * LANES and DIFF_BAND % DIFF_CHUNK == 0 and seq % DIFF_CHUNK == 0 and seq % tq == 0
    ksteps = jnp.asarray(_diff_key_steps(seq), BF16)
    return pl.pallas_call(
        functools.partial(_diff_kernel, seq=seq, lambda_init=lambda_init),
        grid=(nh, bsz, nq),
        in_specs=[pl.BlockSpec(memory_space=pltpu.SMEM),
                  pl.BlockSpec((4, DIFF_QK_DIM), lambda h, b, i: (0, 0)),
                  pl.BlockSpec((None, tq, HEAD_DIM), lambda h, b, i: (b, i, h)),
                  pl.BlockSpec((None, seq, HEAD_DIM), lambda h, b, i: (b, 0, nh + h)),
                  pl.BlockSpec((seq, LANES), lambda h, b, i: (0, 0)),
                  pl.BlockSpec((None, None, HEAD_DIM, seq), lambda h, b, i: (b, h, 0, 0)),
                  pl.BlockSpec((1, DIFF_BAND + 2 * DIFF_FAR, DIFF_QBLK), lambda h, b, i: (h, 0, 0)),
                  pl.BlockSpec((HEAD_DIM, 1), lambda h, b, i: (0, 0))],
        out_specs=pl.BlockSpec((None, tq, HEAD_DIM), lambda h, b, i: (b, i, h)),
        out_shape=jax.ShapeDtypeStruct((bsz, seq, DIFF_WIDTH), BF16),
        scratch_shapes=[pltpu.VMEM((HEAD_DIM, 2 * DIFF_QBLK), F32)],
        compiler_params=_cparams(("parallel", "parallel", "parallel")),
        name="diff_attention",
    )(rel_bias, lam_vecs, proj, proj, ksteps, vt, t_bias, g_subln.reshape(HEAD_DIM, 1))


def kernel(x, rel_bias, g_pre_mix, w_in, lambda_q1, lambda_k1, lambda_q2, lambda_k2, g_subln, w_out,
           g_post_mix, g_pre_mlp, w_up, w_down, g_post_mlp):
    bsz, seq, d = x.shape
    m = bsz * seq
    xf = x.reshape(m, d)

    dil_tiles = _build_bias(rel_bias, _dil_bias_idx(), N_HEADS_DIL, 0)
    dil_tiles = dil_tiles.reshape(N_HEADS_DIL, 3 * len(DILATIONS), DIL_QBLK, DIL_KWIN)
    diff_t = _build_bias(rel_bias, _diff_bias_idx(seq), N_HEADS_DIFF, N_HEADS_DIL, scale=LOG2E)

    ones = jnp.ones((DIL_WIDTH,), F32)
    scale_dil = jnp.concatenate([ones * (HEAD_DIM ** -0.5), ones, ones])
    scale_diff = jnp.concatenate([ones * (DIFF_QK_DIM ** -0.5 * LOG2E), ones, ones])

    h = _rmsnorm(xf, g_pre_mix[0])
    for l in range(DEPTH):
        lambda_init = 0.8 - 0.6 * math.exp(-0.3 * l)
        proj_a = _matmul(h, w_in, l, F32, colscale=scale_dil, b_cols=(0, 3 * DIL_WIDTH))
        proj_d = _matmul(h, w_in, l, BF16, colscale=scale_diff, b_cols=(3 * DIL_WIDTH, 3 * DIFF_WIDTH))
        out_a = _dil_attention(proj_a.reshape(bsz, seq, 3 * DIL_WIDTH), dil_tiles, seq)
        lam_vecs = jnp.stack([lambda_q1[l], lambda_k1[l], lambda_q2[l], lambda_k2[l]]).astype(F32)
        proj_d = proj_d.reshape(bsz, seq, 3 * DIFF_WIDTH)
        vt = proj_d[:, :, 2 * DIFF_WIDTH:].reshape(bsz, seq, N_HEADS_DIFF, HEAD_DIM).transpose(0, 2, 3, 1)
        out_d = _diff_attention(rel_bias, proj_d, vt, diff_t, lam_vecs, g_subln[l], seq, lambda_init)
        y = _matmul((out_a.reshape(m, DIL_WIDTH), out_d.reshape(m, DIFF_WIDTH)), w_out, l, F32,
                    tm=1024, tk=DIL_WIDTH)
        xf, h = _norm_res(xf, y, g_post_mix[l], g_pre_mlp[l])
        u = _matmul(h, w_up, l, BF16, act="relu2")
        y = _matmul(u, w_down, l, F32)
        if l + 1 < DEPTH:
            xf, h = _norm_res(xf, y, g_post_mlp[l], g_pre_mix[l + 1])
        else:
            xf = _norm_res(xf, y, g_post_mlp[l])
    return xf.reshape(bsz, seq, d)
```

```python
import functools
import math

import numpy as np
import jax
import jax.numpy as jnp
from jax import lax
from jax.experimental import pallas as pl
from jax.experimental.pallas import tpu as pltpu

D_MODEL = 4096
DEPTH = 2
HEAD_DIM = 128
N_HEADS_DIL = 16
N_HEADS_DIFF = 16
DIFF_QK_DIM = 64
DIL_WIDTH = N_HEADS_DIL * HEAD_DIM
DIFF_WIDTH = N_HEADS_DIFF * HEAD_DIM
DILATIONS = (16, 4, 1)
DIL_RADIUS = 64
NUM_BUCKETS = 32
MAX_DISTANCE = 1024
RMS_EPS = 1e-6
SUBLN_EPS = 1e-5
NEG_INF = -1e30

LANES = 128
DIL_QBLK = 128
DIL_KWIN = 256
DIL_UNROLL = 8
DIFF_QBLK = 512
DIFF_SUBS = 2
DIFF_FAR = 768
DIFF_BAND = 2 * DIFF_FAR + DIFF_QBLK
DIFF_CHUNK = 512
DIFF_MAX_DENOM = 2.0 ** 80
LOG2E = 1.4426950408889634
VMEM_LIMIT = 56 * 1024 * 1024

F32 = jnp.float32
BF16 = jnp.bfloat16


def _cparams(sem, flags=None):
    return pltpu.CompilerParams(dimension_semantics=sem, vmem_limit_bytes=VMEM_LIMIT, flags=flags)


def _rmsnorm_kernel(x_ref, g_ref, o_ref):
    x = x_ref[...]
    ms = jnp.mean(x * x, axis=-1, keepdims=True)
    o_ref[...] = (x * lax.rsqrt(ms + RMS_EPS) * g_ref[...]).astype(o_ref.dtype)


def _rmsnorm(x, g, tm=256):
    m, d = x.shape
    return pl.pallas_call(
        _rmsnorm_kernel,
        grid=(m // tm,),
        in_specs=[pl.BlockSpec((tm, d), lambda i: (i, 0)),
                  pl.BlockSpec((1, d), lambda i: (0, 0))],
        out_specs=pl.BlockSpec((tm, d), lambda i: (i, 0)),
        out_shape=jax.ShapeDtypeStruct((m, d), BF16),
        compiler_params=_cparams(("parallel",)),
        name="rmsnorm",
    )(x, g.reshape(1, d))


def _mm_epilogue(r, s_ref, act, dtype):
    if s_ref is not None:
        r = r * s_ref[...]
    if act == "relu2":
        r = jnp.square(jnp.maximum(r, 0.0))
    return r.astype(dtype)


def _mm_kernel(*refs, nk, act, scaled, split_a):
    n_a = 2 if split_a else 1
    a_refs = refs[:n_a]
    b_ref = refs[n_a]
    s_ref = refs[n_a + 1] if scaled else None
    o_ref = refs[n_a + 1 + scaled]
    rest = refs[n_a + 2 + scaled:]

    def partial_product(a_ref):
        return jnp.dot(a_ref[...], b_ref[...].astype(BF16), preferred_element_type=F32)

    if nk == 1:
        o_ref[...] = _mm_epilogue(partial_product(a_refs[0]), s_ref, act, o_ref.dtype)
        return
    acc_ref, = rest
    k = pl.program_id(2)

    @pl.when(k == 0)
    def _():
        acc_ref[...] = partial_product(a_refs[0])

    @pl.when(jnp.logical_and(k > 0, k < nk - 1))
    def _():
        acc_ref[...] += partial_product(a_refs[0])

    @pl.when(k == nk - 1)
    def _():
        o_ref[...] = _mm_epilogue(acc_ref[...] + partial_product(a_refs[-1]), s_ref, act, o_ref.dtype)


def _matmul(a, w, layer, out_dtype, colscale=None, act=None, b_cols=None, tm=2048, tn=1024, tk=1024):
    split_a = isinstance(a, tuple)
    a_list = list(a) if split_a else [a]
    m = a_list[0].shape[0]
    kdim = sum(x.shape[1] for x in a_list)
    c0, n = b_cols if b_cols is not None else (0, w.shape[2])
    jb = c0 // tn
    assert c0 % tn == 0 and n % tn == 0 and m % tm == 0 and kdim % tk == 0
    nk = kdim // tk
    scaled = colscale is not None
    if split_a:
        assert nk == 2 and all(x.shape == (m, tk) for x in a_list)
        in_specs = [pl.BlockSpec((tm, tk), lambda i, j, k: (i, 0), pipeline_mode=pl.Buffered(1))
                    for _ in a_list]
    else:
        in_specs = [pl.BlockSpec((tm, tk), lambda i, j, k: (i, k))]
    in_specs.append(pl.BlockSpec((None, tk, tn), lambda i, j, k: (layer, k, j + jb)))
    args = a_list + [w]
    if scaled:
        in_specs.append(pl.BlockSpec((1, tn), lambda i, j, k: (0, j)))
        args.append(colscale.reshape(1, n).astype(F32))
    scratch = [pltpu.VMEM((tm, tn), F32)] if nk > 1 else []
    return pl.pallas_call(
        functools.partial(_mm_kernel, nk=nk, act=act, scaled=scaled, split_a=split_a),
        grid=(m // tm, n // tn, nk),
        in_specs=in_specs,
        out_specs=pl.BlockSpec((tm, tn), lambda i, j, k: (i, j)),
        out_shape=jax.ShapeDtypeStruct((m, n), out_dtype),
        scratch_shapes=scratch,
        compiler_params=_cparams(("parallel", "parallel", "arbitrary")),
        name="matmul",
    )(*args)


def _matmul_wres(a, w, layer, out_dtype, colscale=None, act=None, b_cols=None, tm=512, tn=1024):
    m, kdim = a.shape
    c0, n = b_cols if b_cols is not None else (0, w.shape[2])
    jb = c0 // tn
    assert c0 % tn == 0 and n % tn == 0 and m % tm == 0
    scaled = colscale is not None
    in_specs = [pl.BlockSpec((tm, kdim), lambda j, i: (i, 0)),
                pl.BlockSpec((None, kdim, tn), lambda j, i: (layer, 0, j + jb))]
    args = [a, w]
    if scaled:
        in_specs.append(pl.BlockSpec((1, tn), lambda j, i: (0, j)))
        args.append(colscale.reshape(1, n).astype(F32))
    return pl.pallas_call(
        functools.partial(_mm_kernel, nk=1, act=act, scaled=scaled, split_a=False),
        grid=(n // tn, m // tm),
        in_specs=in_specs,
        out_specs=pl.BlockSpec((tm, tn), lambda j, i: (i, j)),
        out_shape=jax.ShapeDtypeStruct((m, n), out_dtype),
        compiler_params=_cparams(("parallel", "arbitrary")),
        name="matmul_wres",
    )(*args)


def _norm_res_kernel(x_ref, y_ref, g_ref, *rest, with_next):
    y = y_ref[...].astype(F32)
    ms = jnp.mean(y * y, axis=-1, keepdims=True)
    xn = x_ref[...] + y * lax.rsqrt(ms + RMS_EPS) * g_ref[...]
    if with_next:
        g2_ref, o_ref, h_ref = rest
        o_ref[...] = xn
        ms2 = jnp.mean(xn * xn, axis=-1, keepdims=True)
        h_ref[...] = (xn * lax.rsqrt(ms2 + RMS_EPS) * g2_ref[...]).astype(h_ref.dtype)
    else:
        o_ref, = rest
        o_ref[...] = xn


def _norm_res(x, y, g, g_next=None, tm=128):
    m, d = x.shape
    with_next = g_next is not None
    row = pl.BlockSpec((tm, d), lambda i: (i, 0))
    vec = pl.BlockSpec((1, d), lambda i: (0, 0))
    in_specs = [row, row, vec]
    args = [x, y, g.reshape(1, d)]
    out_shape = jax.ShapeDtypeStruct((m, d), F32)
    out_specs = row
    if with_next:
        in_specs.append(vec)
        args.append(g_next.reshape(1, d))
        out_shape = (out_shape, jax.ShapeDtypeStruct((m, d), BF16))
        out_specs = (row, row)
    return pl.pallas_call(
        functools.partial(_norm_res_kernel, with_next=with_next),
        grid=(m // tm,),
        in_specs=in_specs,
        out_specs=out_specs,
        out_shape=out_shape,
        compiler_params=_cparams(("parallel",)),
        name="norm_res",
    )(*args)


def _t5_bucket_np(rel):
    half = NUM_BUCKETS // 2
    max_exact = half // 2
    ret = np.where(rel > 0, half, 0)
    n = np.abs(rel)
    nf = np.maximum(n, 1).astype(np.float32)
    large = max_exact + (np.log(nf / np.float32(max_exact)) / np.float32(math.log(MAX_DISTANCE / max_exact))
                         * np.float32(half - max_exact)).astype(np.int32)
    large = np.minimum(large, half - 1)
    return (ret + np.where(n < max_exact, n, large)).astype(np.int32)


def _bias_kernel(tab_ref, idx_ref, o_ref, *, col0, scale, rb, present):
    h = pl.program_id(0)
    for i, buckets in enumerate(present):
        idx = idx_ref[i * rb:(i + 1) * rb, :]
        acc = jnp.full(idx.shape, NEG_INF, F32) if -1 in buckets else None
        for b in buckets:
            if b < 0:
                continue
            val = tab_ref[b, col0 + h] * scale
            acc = jnp.full(idx.shape, val, F32) if acc is None else jnp.where(idx == b, val, acc)
        o_ref[0, i * rb:(i + 1) * rb, :] = acc


def _build_bias(rel_bias, idx, n_heads, col0, rb=128, scale=1.0):
    r, c = idx.shape
    present = tuple(tuple(int(b) for b in np.unique(idx[i:i + rb])) for i in range(0, r, rb))
    return pl.pallas_call(
        functools.partial(_bias_kernel, col0=col0, scale=scale, rb=rb, present=present),
        grid=(n_heads,),
        in_specs=[pl.BlockSpec(memory_space=pltpu.SMEM),
                  pl.BlockSpec((r, c), lambda h: (0, 0))],
        out_specs=pl.BlockSpec((1, r, c), lambda h: (h, 0, 0)),
        out_shape=jax.ShapeDtypeStruct((n_heads, r, c), F32),
        compiler_params=_cparams(("parallel",)),
        name="bias_tiles",
    )(rel_bias, jnp.asarray(idx))


def _dil_bias_idx():
    row = np.arange(DIL_QBLK)[:, None]
    col = np.arange(DIL_KWIN)[None, :]
    tiles = []
    for d in DILATIONS:
        for off in (0, -DIL_RADIUS, -2 * DIL_RADIUS):
            rel = off + col - row
            idx = _t5_bucket_np(rel * d)
            tiles.append(np.where(np.abs(rel) <= DIL_RADIUS, idx, -1))
    return np.concatenate(tiles, axis=0).astype(np.int32)


def _diff_bias_idx(seq):
    far = np.arange(DIFF_FAR, seq)
    assert np.all(_t5_bucket_np(-far) == NUM_BUCKETS // 2 - 1) and np.all(_t5_bucket_np(far) == NUM_BUCKETS - 1)
    row = np.arange(DIFF_BAND + 2 * DIFF_FAR)[:, None]
    col = np.arange(DIFF_QBLK)[None, :]
    return _t5_bucket_np(row - 2 * DIFF_FAR - col)


def _diff_key_steps(seq):
    key = np.arange(seq)[:, None]
    lane = np.arange(LANES)[None, :]
    return np.where((lane < 64) & (key >= LANES * (lane & 31)), 1.0, 0.0).astype(np.float32)


def _dil_kernel(q_ref, k_ref, v_ref, bias_ref, o_ref, m_scr, l_scr, n_scr, *, seq):
    for bi, d in enumerate(DILATIONS):
        sub_len = seq // d
        qblk = sub_len if sub_len == DIL_KWIN else DIL_QBLK
        nblk = sub_len // qblk
        unroll = DIL_UNROLL * DIL_QBLK // qblk
        log_d = d.bit_length() - 1
        first = bi == 0
        last = bi == len(DILATIONS) - 1

        def rows(start, size, d=d):
            if d == 1:
                return pl.ds(pl.multiple_of(start, DIL_QBLK // 2), size)
            return pl.ds(start, size, stride=d)

        def body(grp, carry, d=d, nblk=nblk, log_d=log_d, first=first, last=last, bi=bi,
                 sub_len=sub_len, rows=rows, qblk=qblk, unroll=unroll):
            qsls, scores, vbs = [], [], []
            for u in range(unroll):
                it = grp * unroll + u
                r = it & (d - 1)
                j = it >> log_d
                q0 = j * qblk
                ws = jnp.clip(q0 - DIL_RADIUS, 0, sub_len - DIL_KWIN)
                if nblk == 1:
                    bias = jnp.concatenate([bias_ref[0, bi * 3], bias_ref[0, bi * 3 + 2]], axis=0)
                else:
                    bias = bias_ref[0, bi * 3 + jnp.where(j == 0, 0, jnp.where(j == nblk - 1, 2, 1))]
                qsl = rows(r + d * q0, qblk)
                ksl = rows(r + d * ws, DIL_KWIN)
                qb = q_ref[qsl, :].astype(BF16)
                kb = k_ref[ksl, :].astype(BF16)
                s = lax.dot_general(qb, kb, (((1,), (1,)), ((), ())), preferred_element_type=F32)
                qsls.append(qsl)
                scores.append(s + bias)
                vbs.append(v_ref[ksl, :].astype(BF16))
            stats = []
            for s in scores:
                m = jnp.max(s, axis=1, keepdims=True)
                p = jnp.exp2(s - m)
                stats.append((m, jnp.sum(p, axis=1, keepdims=True), p.astype(BF16)))
            nums = [jnp.dot(p, vb, preferred_element_type=F32) for (_, _, p), vb in zip(stats, vbs)]
            shape = (qblk, LANES)
            for qsl, (m, l, _), n in zip(qsls, stats, nums):
                m = jnp.broadcast_to(m, shape)
                l = jnp.broadcast_to(l, shape)
                if not first:
                    m_old = m_scr[qsl, :]
                    m_new = jnp.maximum(m_old, m)
                    a = jnp.exp2(m_old - m_new)
                    b = jnp.exp2(m - m_new)
                    l = a * l_scr[qsl, :] + b * l
                    n = a * n_scr[qsl, :] + b * n
                    m = m_new
                if last:
                    o_ref[qsl, :] = (n / l).astype(o_ref.dtype)
                else:
                    m_scr[qsl, :] = m
                    l_scr[qsl, :] = l
                    n_scr[qsl, :] = n
            return carry

        lax.fori_loop(0, seq // (qblk * unroll), body, 0)


def _dil_attention(proj, bias_tiles, seq):
    bsz = proj.shape[0]
    nh = N_HEADS_DIL

    def col(off):
        return pl.BlockSpec((None, seq, HEAD_DIM), lambda b, h: (b, 0, off + h))

    state = pltpu.VMEM((seq, LANES), F32)
    return pl.pallas_call(
        functools.partial(_dil_kernel, seq=seq),
        grid=(bsz, nh),
        in_specs=[col(0), col(nh), col(2 * nh),
                  pl.BlockSpec((1, 3 * len(DILATIONS), DIL_QBLK, DIL_KWIN), lambda b, h: (h, 0, 0, 0))],
        out_specs=pl.BlockSpec((None, seq, HEAD_DIM), lambda b, h: (b, 0, h)),
        out_shape=jax.ShapeDtypeStruct((bsz, seq, DIL_WIDTH), BF16),
        scratch_shapes=[state, state, state],
        compiler_params=_cparams(("parallel", "parallel")),
        name="dilated_attention",
    )(proj, proj, proj, bias_tiles)


def _diff_kernel(tab_ref, lam_ref, q_ref, k_ref, ksteps_ref, vt_ref, t_ref, g_ref, o_ref, o_scr,
                 *, seq, lambda_init):
    h = pl.program_id(0)
    n_piece = seq // LANES
    pieces_per_chunk = DIFF_CHUNK // LANES
    chunk_starts = list(range(0, n_piece, pieces_per_chunk))
    width = 2 * DIFF_QBLK

    lam = (jnp.exp(jnp.sum(lam_ref[0:1, :] * lam_ref[1:2, :], axis=1, keepdims=True))
           - jnp.exp(jnp.sum(lam_ref[2:3, :] * lam_ref[3:4, :], axis=1, keepdims=True)) + lambda_init)
    c_left = tab_ref[NUM_BUCKETS // 2 - 1, N_HEADS_DIL + h] * LOG2E
    c_right = tab_ref[NUM_BUCKETS - 1, N_HEADS_DIL + h] * LOG2E
    lane = lax.broadcasted_iota(jnp.int32, (1, LANES), 1)
    lj = lane & 31
    qlane = lax.broadcasted_iota(jnp.int32, (DIFF_QBLK, LANES), 1)

    class SubBlock:
        def __init__(self, sb):
            rows = slice(sb * DIFF_QBLK, (sb + 1) * DIFF_QBLK)
            q0 = (pl.program_id(2) * DIFF_SUBS + sb) * DIFF_QBLK
            ks = jnp.clip(q0 - DIFF_FAR, 0, seq - DIFF_BAND)
            ts = ks - q0 + 2 * DIFF_FAR
            step_lo = lax.shift_right_logical(ks, 7)
            step_hi = step_lo + DIFF_BAND // LANES
            c = (jnp.where(lj == 0, c_left, 0.0) + jnp.where(lj == step_lo, -c_left, 0.0)
                 + jnp.where(lj == step_hi, c_right, 0.0))
            c = jnp.where(lane < 64, c, 0.0)
            c_hi = c.astype(BF16).astype(F32)
            q_extra = jnp.where(lane < 32, c_hi, c - c_hi).astype(BF16)
            q_extra = jnp.broadcast_to(q_extra, (DIFF_QBLK, LANES))
            q = q_ref[rows, :]
            zero = jnp.zeros_like(q)
            q_both = jnp.concatenate(
                [jnp.concatenate([jnp.where(qlane < DIFF_QK_DIM, q, zero), q_extra], axis=1),
                 jnp.concatenate([jnp.where(qlane >= DIFF_QK_DIM, q, zero), q_extra], axis=1)], axis=0)

            def piece_start(g):
                return pl.multiple_of(((step_lo + g) & (n_piece - 1)) * LANES, LANES)

            def scores(g0):
                k_c = jnp.concatenate(
                    [jnp.concatenate([k_ref[pl.ds(piece_start(g0 + i), LANES), :],
                                      ksteps_ref[pl.ds(piece_start(g0 + i), LANES), :]], axis=1)
                     for i in range(pieces_per_chunk)], axis=0)
                return lax.dot_general(k_c, q_both, (((1,), (1,)), ((), ())), preferred_element_type=F32)

            def band_bias(g0, rows_n=DIFF_CHUNK):
                tile = t_ref[0, pl.ds(pl.multiple_of(ts + g0 * LANES, LANES), rows_n), :]
                return jnp.concatenate([tile, tile], axis=1)

            def values_t(g0):
                return jnp.concatenate([vt_ref[:, pl.ds(piece_start(g0 + i), LANES)]
                                        for i in range(pieces_per_chunk)], axis=1)

            def finish(o, l):
                on = o / l
                od = on[:, :DIFF_QBLK] - lam * on[:, DIFF_QBLK:]
                ms = jnp.mean(od * od, axis=0, keepdims=True)
                y = od * lax.rsqrt(ms + SUBLN_EPS) * g_ref[...] * (1.0 - lambda_init)
                o_ref[rows, :] = y.T.astype(o_ref.dtype)

            self.scores, self.band_bias, self.values_t, self.finish = scores, band_bias, values_t, finish

    subs = [SubBlock(sb) for sb in range(DIFF_SUBS)]
    results = []
    s_next = subs[0].scores(0)
    for sb, sub in enumerate(subs):
        s = s_next
        ref = jnp.max(s[:LANES] + sub.band_bias(0, LANES), axis=0, keepdims=True)
        l8 = jnp.zeros((8, width), F32)
        o = None
        for ci, g0 in enumerate(chunk_starts):
            if ci + 1 < len(chunk_starts):
                s_next = sub.scores(chunk_starts[ci + 1])
            elif sb + 1 < len(subs):
                s_next = subs[sb + 1].scores(0)
            if g0 * LANES < DIFF_BAND:
                s = s + sub.band_bias(g0)
            p = jnp.exp2(s - ref)
            l8 = l8 + jnp.sum(p.reshape(DIFF_CHUNK // 8, 8, width), axis=0)
            o_c = jnp.dot(sub.values_t(g0), p.astype(BF16), preferred_element_type=F32)
            o = o_c if o is None else o + o_c
            s = s_next
        l = jnp.sum(l8, axis=0, keepdims=True)
        sub.finish(o, l)
        results.append((o, l))

    def all_finite(x):
        return jnp.min(jnp.where(jnp.isfinite(x), 1.0, 0.0))

    for sub, (o, l) in zip(subs, results):
        @pl.when(jnp.logical_or(jnp.minimum(all_finite(o), all_finite(l)) < 0.5, jnp.max(l) > DIFF_MAX_DENOM))
        def _(sub=sub):
            def chunk(c, carry, in_band):
                m_old, l_old = carry
                g0 = c * pieces_per_chunk
                sc = sub.scores(g0)
                if in_band:
                    sc = sc + sub.band_bias(g0)
                m_new = jnp.maximum(m_old, jnp.max(sc, axis=0, keepdims=True))
                alpha = jnp.exp2(m_old - m_new)
                pc = jnp.exp2(sc - m_new)
                o_scr[...] = alpha * o_scr[...] + jnp.dot(sub.values_t(g0), pc.astype(BF16),
                                                          preferred_element_type=F32)
                return m_new, alpha * l_old + jnp.sum(pc, axis=0, keepdims=True)

            o_scr[...] = jnp.zeros_like(o_scr)
            carry = (jnp.full((1, width), NEG_INF, F32), jnp.zeros((1, width), F32))
            n_band = DIFF_BAND // DIFF_CHUNK
            carry = lax.fori_loop(0, n_band, functools.partial(chunk, in_band=True), carry)
            _, l_slow = lax.fori_loop(n_band, len(chunk_starts), functools.partial(chunk, in_band=False), carry)
            sub.finish(o_scr[...], l_slow)


def _diff_attention(rel_bias, proj, vt, t_bias, lam_vecs, g_subln, seq, lambda_init):
    bsz = proj.shape[0]
    nh = N_HEADS_DIFF
    tq = DIFF_SUBS * DIFF_QBLK
    nq = seq // tq
    assert seq == 32 * LANES and DIFF_BAND % DIFF_CHUNK == 0 and seq % DIFF_CHUNK == 0 and seq % tq == 0
    ksteps = jnp.asarray(_diff_key_steps(seq), BF16)
    return pl.pallas_call(
        functools.partial(_diff_kernel, seq=seq, lambda_init=lambda_init),
        grid=(nh, bsz, nq),
        in_specs=[pl.BlockSpec(memory_space=pltpu.SMEM),
                  pl.BlockSpec((4, DIFF_QK_DIM), lambda h, b, i: (0, 0)),
                  pl.BlockSpec((None, tq, HEAD_DIM), lambda h, b, i: (b, i, h)),
                  pl.BlockSpec((None, seq, HEAD_DIM), lambda h, b, i: (b, 0, nh + h)),
                  pl.BlockSpec((seq, LANES), lambda h, b, i: (0, 0)),
                  pl.BlockSpec((None, None, HEAD_DIM, seq), lambda h, b, i: (b, h, 0, 0)),
                  pl.BlockSpec((1, DIFF_BAND + 2 * DIFF_FAR, DIFF_QBLK), lambda h, b, i: (h, 0, 0)),
                  pl.BlockSpec((HEAD_DIM, 1), lambda h, b, i: (0, 0))],
        out_specs=pl.BlockSpec((None, tq, HEAD_DIM), lambda h, b, i: (b, i, h)),
        out_shape=jax.ShapeDtypeStruct((bsz, seq, DIFF_WIDTH), BF16),
        scratch_shapes=[pltpu.VMEM((HEAD_DIM, 2 * DIFF_QBLK), F32)],
        compiler_params=_cparams(("parallel", "parallel", "parallel")),
        name="diff_attention",
    )(rel_bias, lam_vecs, proj, proj, ksteps, vt, t_bias, g_subln.reshape(HEAD_DIM, 1))


def kernel(x, rel_bias, g_pre_mix, w_in, lambda_q1, lambda_k1, lambda_q2, lambda_k2, g_subln, w_out,
           g_post_mix, g_pre_mlp, w_up, w_down, g_post_mlp):
    bsz, seq, d = x.shape
    m = bsz * seq
    xf = x.reshape(m, d)

    dil_tiles = _build_bias(rel_bias, _dil_bias_idx(), N_HEADS_DIL, 0, scale=LOG2E)
    dil_tiles = dil_tiles.reshape(N_HEADS_DIL, 3 * len(DILATIONS), DIL_QBLK, DIL_KWIN)
    diff_t = _build_bias(rel_bias, _diff_bias_idx(seq), N_HEADS_DIFF, N_HEADS_DIL, scale=LOG2E)

    ones = jnp.ones((DIL_WIDTH,), F32)
    scale_dil = jnp.concatenate([ones * (HEAD_DIM ** -0.5 * LOG2E), ones, ones])
    scale_diff = jnp.concatenate([ones * (DIFF_QK_DIM ** -0.5 * LOG2E), ones, ones])

    h = _rmsnorm(xf, g_pre_mix[0])
    for l in range(DEPTH):
        lambda_init = 0.8 - 0.6 * math.exp(-0.3 * l)
        proj_a = _matmul(h, w_in, l, F32, colscale=scale_dil, b_cols=(0, 3 * DIL_WIDTH))
        proj_d = _matmul_wres(h, w_in, l, BF16, colscale=scale_diff, b_cols=(3 * DIL_WIDTH, 3 * DIFF_WIDTH))
        out_a = _dil_attention(proj_a.reshape(bsz, seq, 3 * DIL_WIDTH), dil_tiles, seq)
        lam_vecs = jnp.stack([lambda_q1[l], lambda_k1[l], lambda_q2[l], lambda_k2[l]]).astype(F32)
        proj_d = proj_d.reshape(bsz, seq, 3 * DIFF_WIDTH)
        vt = proj_d[:, :, 2 * DIFF_WIDTH:].reshape(bsz, seq, N_HEADS_DIFF, HEAD_DIM).transpose(0, 2, 3, 1)
        out_d = _diff_attention(rel_bias, proj_d, vt, diff_t, lam_vecs, g_subln[l], seq, lambda_init)
        y = _matmul((out_a.reshape(m, DIL_WIDTH), out_d.reshape(m, DIFF_WIDTH)), w_out, l, BF16,
                    tm=1024, tk=DIL_WIDTH)
        xf, h = _norm_res(xf, y, g_post_mix[l], g_pre_mlp[l])
        u = _matmul_wres(h, w_up, l, BF16, act="relu2")
        y = _matmul(u, w_down, l, BF16)
        if l + 1 < DEPTH:
            xf, h = _norm_res(xf, y, g_post_mlp[l], g_pre_mix[l + 1])
        else:
            xf = _norm_res(xf, y, g_post_mlp[l])
    return xf.reshape(bsz, seq, d)
```

```python
import functools
import math

import numpy as np
import jax
import jax.numpy as jnp
from jax import lax
from jax.experimental import pallas as pl
from jax.experimental.pallas import tpu as pltpu

D_MODEL = 4096
DEPTH = 2
HEAD_DIM = 128
N_HEADS_DIL = 16
N_HEADS_DIFF = 16
DIFF_QK_DIM = 64
DIL_WIDTH = N_HEADS_DIL * HEAD_DIM
DIFF_WIDTH = N_HEADS_DIFF * HEAD_DIM
DILATIONS = (16, 4, 1)
DIL_RADIUS = 64
NUM_BUCKETS = 32
MAX_DISTANCE = 1024
RMS_EPS = 1e-6
SUBLN_EPS = 1e-5
NEG_INF = -1e30

LANES = 128
DIL_QBLK = 128
DIL_KWIN = 256
DIL_UNROLL = 8
DIFF_QBLK = 512
DIFF_SUBS = 2
DIFF_FAR = 768
DIFF_BAND = 2 * DIFF_FAR + DIFF_QBLK
DIFF_CHUNK = 512
DIFF_MAX_DENOM = 2.0 ** 80
LOG2E = 1.4426950408889634
VMEM_LIMIT = 58 * 1024 * 1024

F32 = jnp.float32
BF16 = jnp.bfloat16


def _cparams(sem, flags=None):
    return pltpu.CompilerParams(dimension_semantics=sem, vmem_limit_bytes=VMEM_LIMIT, flags=flags)


def _rmsnorm_kernel(x_ref, g_ref, o_ref):
    x = x_ref[...]
    ms = jnp.mean(x * x, axis=-1, keepdims=True)
    o_ref[...] = (x * lax.rsqrt(ms + RMS_EPS) * g_ref[...]).astype(o_ref.dtype)


def _rmsnorm(x, g, tm=256):
    m, d = x.shape
    return pl.pallas_call(
        _rmsnorm_kernel,
        grid=(m // tm,),
        in_specs=[pl.BlockSpec((tm, d), lambda i: (i, 0)),
                  pl.BlockSpec((1, d), lambda i: (0, 0))],
        out_specs=pl.BlockSpec((tm, d), lambda i: (i, 0)),
        out_shape=jax.ShapeDtypeStruct((m, d), BF16),
        compiler_params=_cparams(("parallel",)),
        name="rmsnorm",
    )(x, g.reshape(1, d))


def _mm_epilogue(r, s_ref, act, dtype):
    if s_ref is not None:
        r = r * s_ref[...]
    if act == "relu2":
        r = jnp.square(jnp.maximum(r, 0.0))
    return r.astype(dtype)


def _mm_kernel(*refs, nk, act, scaled, split_a):
    n_a = 2 if split_a else 1
    a_refs = refs[:n_a]
    b_ref = refs[n_a]
    s_ref = refs[n_a + 1] if scaled else None
    o_ref = refs[n_a + 1 + scaled]
    rest = refs[n_a + 2 + scaled:]

    def partial_product(a_ref):
        return jnp.dot(a_ref[...], b_ref[...].astype(BF16), preferred_element_type=F32)

    if nk == 1:
        o_ref[...] = _mm_epilogue(partial_product(a_refs[0]), s_ref, act, o_ref.dtype)
        return
    acc_ref, = rest
    k = pl.program_id(2)

    @pl.when(k == 0)
    def _():
        acc_ref[...] = partial_product(a_refs[0])

    @pl.when(jnp.logical_and(k > 0, k < nk - 1))
    def _():
        acc_ref[...] += partial_product(a_refs[0])

    @pl.when(k == nk - 1)
    def _():
        o_ref[...] = _mm_epilogue(acc_ref[...] + partial_product(a_refs[-1]), s_ref, act, o_ref.dtype)


def _matmul(a, w, layer, out_dtype, colscale=None, act=None, b_cols=None, tm=2048, tn=1024, tk=1024):
    split_a = isinstance(a, tuple)
    a_list = list(a) if split_a else [a]
    m = a_list[0].shape[0]
    kdim = sum(x.shape[1] for x in a_list)
    c0, n = b_cols if b_cols is not None else (0, w.shape[2])
    jb = c0 // tn
    assert c0 % tn == 0 and n % tn == 0 and m % tm == 0 and kdim % tk == 0
    nk = kdim // tk
    scaled = colscale is not None
    if split_a:
        assert nk == 2 and all(x.shape == (m, tk) for x in a_list)
        in_specs = [pl.BlockSpec((tm, tk), lambda i, j, k: (i, 0), pipeline_mode=pl.Buffered(1))
                    for _ in a_list]
    else:
        in_specs = [pl.BlockSpec((tm, tk), lambda i, j, k: (i, k))]
    in_specs.append(pl.BlockSpec((None, tk, tn), lambda i, j, k: (layer, k, j + jb)))
    args = a_list + [w]
    if scaled:
        in_specs.append(pl.BlockSpec((1, tn), lambda i, j, k: (0, j)))
        args.append(colscale.reshape(1, n).astype(F32))
    scratch = [pltpu.VMEM((tm, tn), F32)] if nk > 1 else []
    return pl.pallas_call(
        functools.partial(_mm_kernel, nk=nk, act=act, scaled=scaled, split_a=split_a),
        grid=(m // tm, n // tn, nk),
        in_specs=in_specs,
        out_specs=pl.BlockSpec((tm, tn), lambda i, j, k: (i, j)),
        out_shape=jax.ShapeDtypeStruct((m, n), out_dtype),
        scratch_shapes=scratch,
        compiler_params=_cparams(("parallel", "parallel", "arbitrary")),
        name="matmul",
    )(*args)


def _mm_wres_kernel(*refs, act, scaled):
    if scaled:
        a_ref, b_ref, s_ref, o_ref, b16_ref = refs
    else:
        a_ref, b_ref, o_ref, b16_ref = refs
        s_ref = None

    @pl.when(pl.program_id(1) == 0)
    def _():
        b16_ref[...] = b_ref[...].astype(BF16)

    r = jnp.dot(a_ref[...], b16_ref[...], preferred_element_type=F32)
    o_ref[...] = _mm_epilogue(r, s_ref, act, o_ref.dtype)


def _matmul_wres(a, w, layer, out_dtype, colscale=None, act=None, b_cols=None, tm=512, tn=1024):
    m, kdim = a.shape
    c0, n = b_cols if b_cols is not None else (0, w.shape[2])
    jb = c0 // tn
    assert c0 % tn == 0 and n % tn == 0 and m % tm == 0
    scaled = colscale is not None
    in_specs = [pl.BlockSpec((tm, kdim), lambda j, i: (i, 0)),
                pl.BlockSpec((None, kdim, tn), lambda j, i: (layer, 0, j + jb))]
    args = [a, w]
    if scaled:
        in_specs.append(pl.BlockSpec((1, tn), lambda j, i: (0, j)))
        args.append(colscale.reshape(1, n).astype(F32))
    return pl.pallas_call(
        functools.partial(_mm_wres_kernel, act=act, scaled=scaled),
        grid=(n // tn, m // tm),
        in_specs=in_specs,
        out_specs=pl.BlockSpec((tm, tn), lambda j, i: (i, j)),
        out_shape=jax.ShapeDtypeStruct((m, n), out_dtype),
        scratch_shapes=[pltpu.VMEM((kdim, tn), BF16)],
        compiler_params=_cparams(("parallel", "arbitrary")),
        name="matmul_wres",
    )(*args)


def _norm_res_kernel(x_ref, y_ref, g_ref, *rest, with_next):
    y = y_ref[...].astype(F32)
    ms = jnp.mean(y * y, axis=-1, keepdims=True)
    xn = x_ref[...] + y * lax.rsqrt(ms + RMS_EPS) * g_ref[...]
    if with_next:
        g2_ref, o_ref, h_ref = rest
        o_ref[...] = xn
        ms2 = jnp.mean(xn * xn, axis=-1, keepdims=True)
        h_ref[...] = (xn * lax.rsqrt(ms2 + RMS_EPS) * g2_ref[...]).astype(h_ref.dtype)
    else:
        o_ref, = rest
        o_ref[...] = xn


def _norm_res(x, y, g, g_next=None, tm=128):
    m, d = x.shape
    with_next = g_next is not None
    row = pl.BlockSpec((tm, d), lambda i: (i, 0))
    vec = pl.BlockSpec((1, d), lambda i: (0, 0))
    in_specs = [row, row, vec]
    args = [x, y, g.reshape(1, d)]
    out_shape = jax.ShapeDtypeStruct((m, d), F32)
    out_specs = row
    if with_next:
        in_specs.append(vec)
        args.append(g_next.reshape(1, d))
        out_shape = (out_shape, jax.ShapeDtypeStruct((m, d), BF16))
        out_specs = (row, row)
    return pl.pallas_call(
        functools.partial(_norm_res_kernel, with_next=with_next),
        grid=(m // tm,),
        in_specs=in_specs,
        out_specs=out_specs,
        out_shape=out_shape,
        compiler_params=_cparams(("parallel",)),
        name="norm_res",
    )(*args)


def _t5_bucket_np(rel):
    half = NUM_BUCKETS // 2
    max_exact = half // 2
    ret = np.where(rel > 0, half, 0)
    n = np.abs(rel)
    nf = np.maximum(n, 1).astype(np.float32)
    large = max_exact + (np.log(nf / np.float32(max_exact)) / np.float32(math.log(MAX_DISTANCE / max_exact))
                         * np.float32(half - max_exact)).astype(np.int32)
    large = np.minimum(large, half - 1)
    return (ret + np.where(n < max_exact, n, large)).astype(np.int32)


def _bias_kernel(tab_ref, idx_ref, o_ref, *, col0, scale, rb, present):
    h = pl.program_id(0)
    for i, buckets in enumerate(present):
        idx = idx_ref[i * rb:(i + 1) * rb, :]
        acc = jnp.full(idx.shape, NEG_INF, F32) if -1 in buckets else None
        for b in buckets:
            if b < 0:
                continue
            val = tab_ref[b, col0 + h] * scale
            acc = jnp.full(idx.shape, val, F32) if acc is None else jnp.where(idx == b, val, acc)
        o_ref[0, i * rb:(i + 1) * rb, :] = acc


def _build_bias(rel_bias, idx, n_heads, col0, rb=128, scale=1.0):
    r, c = idx.shape
    present = tuple(tuple(int(b) for b in np.unique(idx[i:i + rb])) for i in range(0, r, rb))
    return pl.pallas_call(
        functools.partial(_bias_kernel, col0=col0, scale=scale, rb=rb, present=present),
        grid=(n_heads,),
        in_specs=[pl.BlockSpec(memory_space=pltpu.SMEM),
                  pl.BlockSpec((r, c), lambda h: (0, 0))],
        out_specs=pl.BlockSpec((1, r, c), lambda h: (h, 0, 0)),
        out_shape=jax.ShapeDtypeStruct((n_heads, r, c), F32),
        compiler_params=_cparams(("parallel",)),
        name="bias_tiles",
    )(rel_bias, jnp.asarray(idx))


def _dil_bias_idx():
    row = np.arange(DIL_QBLK)[:, None]
    col = np.arange(DIL_KWIN)[None, :]
    tiles = []
    for d in DILATIONS:
        for off in (0, -DIL_RADIUS, -2 * DIL_RADIUS):
            rel = off + col - row
            idx = _t5_bucket_np(rel * d)
            tiles.append(np.where(np.abs(rel) <= DIL_RADIUS, idx, -1))
    return np.concatenate(tiles, axis=0).astype(np.int32)


def _diff_bias_idx(seq):
    far = np.arange(DIFF_FAR, seq)
    assert np.all(_t5_bucket_np(-far) == NUM_BUCKETS // 2 - 1) and np.all(_t5_bucket_np(far) == NUM_BUCKETS - 1)
    row = np.arange(DIFF_BAND + 2 * DIFF_FAR)[:, None]
    col = np.arange(DIFF_QBLK)[None, :]
    return _t5_bucket_np(row - 2 * DIFF_FAR - col)


def _diff_key_steps(seq):
    key = np.arange(seq)[:, None]
    lane = np.arange(LANES)[None, :]
    return np.where((lane < 64) & (key >= LANES * (lane & 31)), 1.0, 0.0).astype(np.float32)


def _dil_kernel(q_ref, k_ref, v_ref, bias_ref, o_ref, m_scr, l_scr, n_scr, *, seq):
    for bi, d in enumerate(DILATIONS):
        sub_len = seq // d
        qblk = sub_len if sub_len == DIL_KWIN else DIL_QBLK
        nblk = sub_len // qblk
        unroll = DIL_UNROLL * DIL_QBLK // qblk
        log_d = d.bit_length() - 1
        first = bi == 0
        last = bi == len(DILATIONS) - 1

        def rows(start, size, d=d):
            if d == 1:
                return pl.ds(pl.multiple_of(start, DIL_QBLK // 2), size)
            return pl.ds(start, size, stride=d)

        def body(grp, carry, d=d, nblk=nblk, log_d=log_d, first=first, last=last, bi=bi,
                 sub_len=sub_len, rows=rows, qblk=qblk, unroll=unroll):
            qsls, scores, vbs = [], [], []
            for u in range(unroll):
                it = grp * unroll + u
                r = it & (d - 1)
                j = it >> log_d
                q0 = j * qblk
                ws = jnp.clip(q0 - DIL_RADIUS, 0, sub_len - DIL_KWIN)
                if nblk == 1:
                    bias = jnp.concatenate([bias_ref[0, bi * 3], bias_ref[0, bi * 3 + 2]], axis=0)
                else:
                    bias = bias_ref[0, bi * 3 + jnp.where(j == 0, 0, jnp.where(j == nblk - 1, 2, 1))]
                qsl = rows(r + d * q0, qblk)
                ksl = rows(r + d * ws, DIL_KWIN)
                qb = q_ref[qsl, :].astype(BF16)
                kb = k_ref[ksl, :].astype(BF16)
                s = lax.dot_general(qb, kb, (((1,), (1,)), ((), ())), preferred_element_type=F32)
                qsls.append(qsl)
                scores.append(s + bias)
                vbs.append(v_ref[ksl, :].astype(BF16))
            stats = []
            for s in scores:
                m = jnp.max(s, axis=1, keepdims=True)
                p = jnp.exp2(s - m)
                stats.append((m, jnp.sum(p, axis=1, keepdims=True), p.astype(BF16)))
            nums = [jnp.dot(p, vb, preferred_element_type=F32) for (_, _, p), vb in zip(stats, vbs)]
            shape = (qblk, LANES)
            for qsl, (m, l, _), n in zip(qsls, stats, nums):
                m = jnp.broadcast_to(m, shape)
                l = jnp.broadcast_to(l, shape)
                if not first:
                    m_old = m_scr[qsl, :]
                    m_new = jnp.maximum(m_old, m)
                    a = jnp.exp2(m_old - m_new)
                    b = jnp.exp2(m - m_new)
                    l = a * l_scr[qsl, :] + b * l
                    n = a * n_scr[qsl, :] + b * n
                    m = m_new
                if last:
                    o_ref[qsl, :] = (n / l).astype(o_ref.dtype)
                else:
                    m_scr[qsl, :] = m
                    l_scr[qsl, :] = l
                    n_scr[qsl, :] = n
            return carry

        lax.fori_loop(0, seq // (qblk * unroll), body, 0)


def _dil_attention(proj, bias_tiles, seq):
    bsz = proj.shape[0]
    nh = N_HEADS_DIL

    def col(off):
        return pl.BlockSpec((None, seq, HEAD_DIM), lambda b, h: (b, 0, off + h))

    state = pltpu.VMEM((seq, LANES), F32)
    return pl.pallas_call(
        functools.partial(_dil_kernel, seq=seq),
        grid=(bsz, nh),
        in_specs=[col(0), col(nh), col(2 * nh),
                  pl.BlockSpec((1, 3 * len(DILATIONS), DIL_QBLK, DIL_KWIN), lambda b, h: (h, 0, 0, 0))],
        out_specs=pl.BlockSpec((None, seq, HEAD_DIM), lambda b, h: (b, 0, h)),
        out_shape=jax.ShapeDtypeStruct((bsz, seq, DIL_WIDTH), BF16),
        scratch_shapes=[state, state, state],
        compiler_params=_cparams(("parallel", "parallel")),
        name="dilated_attention",
    )(proj, proj, proj, bias_tiles)


def _diff_kernel(tab_ref, lam_ref, q_ref, k_ref, ksteps_ref, vt_ref, t_ref, g_ref, o_ref, o_scr,
                 *, seq, lambda_init):
    h = pl.program_id(0)
    n_piece = seq // LANES
    pieces_per_chunk = DIFF_CHUNK // LANES
    chunk_starts = list(range(0, n_piece, pieces_per_chunk))
    width = 2 * DIFF_QBLK

    lam = (jnp.exp(jnp.sum(lam_ref[0:1, :] * lam_ref[1:2, :], axis=1, keepdims=True))
           - jnp.exp(jnp.sum(lam_ref[2:3, :] * lam_ref[3:4, :], axis=1, keepdims=True)) + lambda_init)
    c_left = tab_ref[NUM_BUCKETS // 2 - 1, N_HEADS_DIL + h] * LOG2E
    c_right = tab_ref[NUM_BUCKETS - 1, N_HEADS_DIL + h] * LOG2E
    lane = lax.broadcasted_iota(jnp.int32, (1, LANES), 1)
    lj = lane & 31
    qlane = lax.broadcasted_iota(jnp.int32, (DIFF_QBLK, LANES), 1)

    class SubBlock:
        def __init__(self, sb):
            rows = slice(sb * DIFF_QBLK, (sb + 1) * DIFF_QBLK)
            q0 = (pl.program_id(2) * DIFF_SUBS + sb) * DIFF_QBLK
            ks = jnp.clip(q0 - DIFF_FAR, 0, seq - DIFF_BAND)
            ts = ks - q0 + 2 * DIFF_FAR
            step_lo = lax.shift_right_logical(ks, 7)
            step_hi = step_lo + DIFF_BAND // LANES
            c = (jnp.where(lj == 0, c_left, 0.0) + jnp.where(lj == step_lo, -c_left, 0.0)
                 + jnp.where(lj == step_hi, c_right, 0.0))
            c = jnp.where(lane < 64, c, 0.0)
            c_hi = c.astype(BF16).astype(F32)
            q_extra = jnp.where(lane < 32, c_hi, c - c_hi).astype(BF16)
            q_extra = jnp.broadcast_to(q_extra, (DIFF_QBLK, LANES))
            q = q_ref[rows, :]
            zero = jnp.zeros_like(q)
            q_both = jnp.concatenate(
                [jnp.concatenate([jnp.where(qlane < DIFF_QK_DIM, q, zero), q_extra], axis=1),
                 jnp.concatenate([jnp.where(qlane >= DIFF_QK_DIM, q, zero), q_extra], axis=1)], axis=0)

            def piece_start(g):
                return pl.multiple_of(((step_lo + g) & (n_piece - 1)) * LANES, LANES)

            def scores(g0):
                k_c = jnp.concatenate(
                    [jnp.concatenate([k_ref[pl.ds(piece_start(g0 + i), LANES), :],
                                      ksteps_ref[pl.ds(piece_start(g0 + i), LANES), :]], axis=1)
                     for i in range(pieces_per_chunk)], axis=0)
                return lax.dot_general(k_c, q_both, (((1,), (1,)), ((), ())), preferred_element_type=F32)

            def band_bias(g0, rows_n=DIFF_CHUNK):
                tile = t_ref[0, pl.ds(pl.multiple_of(ts + g0 * LANES, LANES), rows_n), :]
                return jnp.concatenate([tile, tile], axis=1)

            def values_t(g0):
                return jnp.concatenate([vt_ref[:, pl.ds(piece_start(g0 + i), LANES)]
                                        for i in range(pieces_per_chunk)], axis=1)

            def finish(o, l):
                on = o / l
                od = on[:, :DIFF_QBLK] - lam * on[:, DIFF_QBLK:]
                ms = jnp.mean(od * od, axis=0, keepdims=True)
                y = od * lax.rsqrt(ms + SUBLN_EPS) * g_ref[...] * (1.0 - lambda_init)
                o_ref[rows, :] = y.T.astype(o_ref.dtype)

            self.scores, self.band_bias, self.values_t, self.finish = scores, band_bias, values_t, finish

    subs = [SubBlock(sb) for sb in range(DIFF_SUBS)]
    results = []
    s_next = subs[0].scores(0)
    for sb, sub in enumerate(subs):
        s = s_next
        ref = jnp.max(s[:LANES] + sub.band_bias(0, LANES), axis=0, keepdims=True)
        l8 = jnp.zeros((8, width), F32)
        o = None
        for ci, g0 in enumerate(chunk_starts):
            if ci + 1 < len(chunk_starts):
                s_next = sub.scores(chunk_starts[ci + 1])
            elif sb + 1 < len(subs):
                s_next = subs[sb + 1].scores(0)
            if g0 * LANES < DIFF_BAND:
                s = s + sub.band_bias(g0)
            p = jnp.exp2(s - ref)
            l8 = l8 + jnp.sum(p.reshape(DIFF_CHUNK // 8, 8, width), axis=0)
            o_c = jnp.dot(sub.values_t(g0), p.astype(BF16), preferred_element_type=F32)
            o = o_c if o is None else o + o_c
            s = s_next
        l = jnp.sum(l8, axis=0, keepdims=True)
        sub.finish(o, l)
        results.append((o, l))

    def all_finite(x):
        return jnp.min(jnp.where(jnp.isfinite(x), 1.0, 0.0))

    for sub, (o, l) in zip(subs, results):
        @pl.when(jnp.logical_or(jnp.minimum(all_finite(o), all_finite(l)) < 0.5, jnp.max(l) > DIFF_MAX_DENOM))
        def _(sub=sub):
            def chunk(c, carry, in_band):
                m_old, l_old = carry
                g0 = c * pieces_per_chunk
                sc = sub.scores(g0)
                if in_band:
                    sc = sc + sub.band_bias(g0)
                m_new = jnp.maximum(m_old, jnp.max(sc, axis=0, keepdims=True))
                alpha = jnp.exp2(m_old - m_new)
                pc = jnp.exp2(sc - m_new)
                o_scr[...] = alpha * o_scr[...] + jnp.dot(sub.values_t(g0), pc.astype(BF16),
                                                          preferred_element_type=F32)
                return m_new, alpha * l_old + jnp.sum(pc, axis=0, keepdims=True)

            o_scr[...] = jnp.zeros_like(o_scr)
            carry = (jnp.full((1, width), NEG_INF, F32), jnp.zeros((1, width), F32))
            n_band = DIFF_BAND // DIFF_CHUNK
            carry = lax.fori_loop(0, n_band, functools.partial(chunk, in_band=True), carry)
            _, l_slow = lax.fori_loop(n_band, len(chunk_starts), functools.partial(chunk, in_band=False), carry)
            sub.finish(o_scr[...], l_slow)


def _diff_attention(rel_bias, proj, vt, t_bias, lam_vecs, g_subln, seq, lambda_init):
    bsz = proj.shape[0]
    nh = N_HEADS_DIFF
    tq = DIFF_SUBS * DIFF_QBLK
    nq = seq // tq
    assert seq == 32 * LANES and DIFF_BAND % DIFF_CHUNK == 0 and seq % DIFF_CHUNK == 0 and seq % tq == 0
    ksteps = jnp.asarray(_diff_key_steps(seq), BF16)
    return pl.pallas_call(
        functools.partial(_diff_kernel, seq=seq, lambda_init=lambda_init),
        grid=(nh, bsz, nq),
        in_specs=[pl.BlockSpec(memory_space=pltpu.SMEM),
                  pl.BlockSpec((4, DIFF_QK_DIM), lambda h, b, i: (0, 0)),
                  pl.BlockSpec((None, tq, HEAD_DIM), lambda h, b, i: (b, i, h)),
                  pl.BlockSpec((None, seq, HEAD_DIM), lambda h, b, i: (b, 0, nh + h)),
                  pl.BlockSpec((seq, LANES), lambda h, b, i: (0, 0)),
                  pl.BlockSpec((None, None, HEAD_DIM, seq), lambda h, b, i: (b, h, 0, 0)),
                  pl.BlockSpec((1, DIFF_BAND + 2 * DIFF_FAR, DIFF_QBLK), lambda h, b, i: (h, 0, 0)),
                  pl.BlockSpec((HEAD_DIM, 1), lambda h, b, i: (0, 0))],
        out_specs=pl.BlockSpec((None, tq, HEAD_DIM), lambda h, b, i: (b, i, h)),
        out_shape=jax.ShapeDtypeStruct((bsz, seq, DIFF_WIDTH), BF16),
        scratch_shapes=[pltpu.VMEM((HEAD_DIM, 2 * DIFF_QBLK), F32)],
        compiler_params=_cparams(("parallel", "parallel", "parallel")),
        name="diff_attention",
    )(rel_bias, lam_vecs, proj, proj, ksteps, vt, t_bias, g_subln.reshape(HEAD_DIM, 1))


def kernel(x, rel_bias, g_pre_mix, w_in, lambda_q1, lambda_k1, lambda_q2, lambda_k2, g_subln, w_out,
           g_post_mix, g_pre_mlp, w_up, w_down, g_post_mlp):
    bsz, seq, d = x.shape
    m = bsz * seq
    xf = x.reshape(m, d)

    dil_tiles = _build_bias(rel_bias, _dil_bias_idx(), N_HEADS_DIL, 0, scale=LOG2E)
    dil_tiles = dil_tiles.reshape(N_HEADS_DIL, 3 * len(DILATIONS), DIL_QBLK, DIL_KWIN)
    diff_t = _build_bias(rel_bias, _diff_bias_idx(seq), N_HEADS_DIFF, N_HEADS_DIL, scale=LOG2E)

    ones = jnp.ones((DIL_WIDTH,), F32)
    scale_dil = jnp.concatenate([ones * (HEAD_DIM ** -0.5 * LOG2E), ones, ones])
    scale_diff = jnp.concatenate([ones * (DIFF_QK_DIM ** -0.5 * LOG2E), ones, ones])

    h = _rmsnorm(xf, g_pre_mix[0])
    for l in range(DEPTH):
        lambda_init = 0.8 - 0.6 * math.exp(-0.3 * l)
        proj_a = _matmul_wres(h, w_in, l, F32, colscale=scale_dil, b_cols=(0, 3 * DIL_WIDTH))
        proj_d = _matmul_wres(h, w_in, l, BF16, colscale=scale_diff, b_cols=(3 * DIL_WIDTH, 3 * DIFF_WIDTH))
        out_a = _dil_attention(proj_a.reshape(bsz, seq, 3 * DIL_WIDTH), dil_tiles, seq)
        lam_vecs = jnp.stack([lambda_q1[l], lambda_k1[l], lambda_q2[l], lambda_k2[l]]).astype(F32)
        proj_d = proj_d.reshape(bsz, seq, 3 * DIFF_WIDTH)
        vt = proj_d[:, :, 2 * DIFF_WIDTH:].reshape(bsz, seq, N_HEADS_DIFF, HEAD_DIM).transpose(0, 2, 3, 1)
        out_d = _diff_attention(rel_bias, proj_d, vt, diff_t, lam_vecs, g_subln[l], seq, lambda_init)
        y = _matmul((out_a.reshape(m, DIL_WIDTH), out_d.reshape(m, DIFF_WIDTH)), w_out, l, BF16,
                    tm=1024, tk=DIL_WIDTH)
        xf, h = _norm_res(xf, y, g_post_mix[l], g_pre_mlp[l])
        u = _matmul_wres(h, w_up, l, BF16, act="relu2")
        y = _matmul(u, w_down, l, BF16)
        if l + 1 < DEPTH:
            xf, h = _norm_res(xf, y, g_post_mlp[l], g_pre_mix[l + 1])
        else:
            xf = _norm_res(xf, y, g_post_mlp[l])
    return xf.reshape(bsz, seq, d)
```

```python
import functools
import math

import numpy as np
import jax
import jax.numpy as jnp
from jax import lax
from jax.experimental import pallas as pl
from jax.experimental.pallas import tpu as pltpu

D_MODEL = 4096
DEPTH = 2
HEAD_DIM = 128
N_HEADS_DIL = 16
N_HEADS_DIFF = 16
DIFF_QK_DIM = 64
DIL_WIDTH = N_HEADS_DIL * HEAD_DIM
DIFF_WIDTH = N_HEADS_DIFF * HEAD_DIM
DILATIONS = (16, 4, 1)
DIL_RADIUS = 64
NUM_BUCKETS = 32
MAX_DISTANCE = 1024
RMS_EPS = 1e-6
SUBLN_EPS = 1e-5
NEG_INF = -1e30

LANES = 128
DIL_QBLK = 128
DIL_KWIN = 256
DIL_UNROLL = 8
DIFF_QBLK = 512
DIFF_SUBS = 2
DIFF_STEPS = 32
DIFF_FAR = 768
DIFF_BAND = 2 * DIFF_FAR + DIFF_QBLK
DIFF_CHUNK = 512
DIFF_MAX_DENOM = 2.0 ** 80
LOG2E = 1.4426950408889634
VMEM_LIMIT = 58 * 1024 * 1024

F32 = jnp.float32
BF16 = jnp.bfloat16


def _cparams(sem, flags=None):
    return pltpu.CompilerParams(dimension_semantics=sem, vmem_limit_bytes=VMEM_LIMIT, flags=flags)


def _rmsnorm_kernel(x_ref, g_ref, o_ref):
    x = x_ref[...]
    ms = jnp.mean(x * x, axis=-1, keepdims=True)
    o_ref[...] = (x * lax.rsqrt(ms + RMS_EPS) * g_ref[...]).astype(o_ref.dtype)


def _rmsnorm(x, g, tm=256):
    m, d = x.shape
    return pl.pallas_call(
        _rmsnorm_kernel,
        grid=(m // tm,),
        in_specs=[pl.BlockSpec((tm, d), lambda i: (i, 0)),
                  pl.BlockSpec((1, d), lambda i: (0, 0))],
        out_specs=pl.BlockSpec((tm, d), lambda i: (i, 0)),
        out_shape=jax.ShapeDtypeStruct((m, d), BF16),
        compiler_params=_cparams(("parallel",)),
        name="rmsnorm",
    )(x, g.reshape(1, d))


def _mm_epilogue(r, s_ref, act, dtype):
    if s_ref is not None:
        r = r * s_ref[...]
    if act == "relu2":
        r = jnp.square(jnp.maximum(r, 0.0))
    return r.astype(dtype)


def _mm_kernel(*refs, nk, act, scaled, split_a):
    n_a = 2 if split_a else 1
    a_refs = refs[:n_a]
    b_ref = refs[n_a]
    s_ref = refs[n_a + 1] if scaled else None
    o_ref = refs[n_a + 1 + scaled]

    acc_ref, = refs[n_a + 2 + scaled:]
    k = pl.program_id(2)

    def partial_product(a_ref):
        return jnp.dot(a_ref[...], b_ref[...].astype(BF16), preferred_element_type=F32)

    @pl.when(k == 0)
    def _():
        acc_ref[...] = partial_product(a_refs[0])

    @pl.when(jnp.logical_and(k > 0, k < nk - 1))
    def _():
        acc_ref[...] += partial_product(a_refs[0])

    @pl.when(k == nk - 1)
    def _():
        o_ref[...] = _mm_epilogue(acc_ref[...] + partial_product(a_refs[-1]), s_ref, act, o_ref.dtype)


def _matmul(a, w, layer, out_dtype, colscale=None, act=None, b_cols=None, tm=2048, tn=1024, tk=1024):
    split_a = isinstance(a, tuple)
    a_list = list(a) if split_a else [a]
    m = a_list[0].shape[0]
    kdim = sum(x.shape[1] for x in a_list)
    c0, n = b_cols if b_cols is not None else (0, w.shape[2])
    jb = c0 // tn
    assert c0 % tn == 0 and n % tn == 0 and m % tm == 0 and kdim % tk == 0
    nk = kdim // tk
    assert nk >= 2
    scaled = colscale is not None
    if split_a:
        assert nk == 2 and all(x.shape == (m, tk) for x in a_list)
        in_specs = [pl.BlockSpec((tm, tk), lambda i, j, k: (i, 0), pipeline_mode=pl.Buffered(1))
                    for _ in a_list]
    else:
        in_specs = [pl.BlockSpec((tm, tk), lambda i, j, k: (i, k))]
    in_specs.append(pl.BlockSpec((None, tk, tn), lambda i, j, k: (layer, k, j + jb)))
    args = a_list + [w]
    if scaled:
        in_specs.append(pl.BlockSpec((1, tn), lambda i, j, k: (0, j)))
        args.append(colscale.reshape(1, n).astype(F32))
    return pl.pallas_call(
        functools.partial(_mm_kernel, nk=nk, act=act, scaled=scaled, split_a=split_a),
        grid=(m // tm, n // tn, nk),
        in_specs=in_specs,
        out_specs=pl.BlockSpec((tm, tn), lambda i, j, k: (i, j)),
        out_shape=jax.ShapeDtypeStruct((m, n), out_dtype),
        scratch_shapes=[pltpu.VMEM((tm, tn), F32)],
        compiler_params=_cparams(("parallel", "parallel", "arbitrary")),
        name="matmul",
    )(*args)


def _mm_wres_kernel(*refs, act, scaled):
    if scaled:
        a_ref, b_ref, s_ref, o_ref, b16_ref = refs
    else:
        a_ref, b_ref, o_ref, b16_ref = refs
        s_ref = None

    @pl.when(pl.program_id(1) == 0)
    def _():
        b16_ref[...] = b_ref[...].astype(BF16)

    r = jnp.dot(a_ref[...], b16_ref[...], preferred_element_type=F32)
    o_ref[...] = _mm_epilogue(r, s_ref, act, o_ref.dtype)


def _matmul_wres(a, w, layer, out_dtype, colscale=None, act=None, b_cols=None, tm=512, tn=1024):
    m, kdim = a.shape
    c0, n = b_cols if b_cols is not None else (0, w.shape[2])
    jb = c0 // tn
    assert c0 % tn == 0 and n % tn == 0 and m % tm == 0
    scaled = colscale is not None
    in_specs = [pl.BlockSpec((tm, kdim), lambda j, i: (i, 0)),
                pl.BlockSpec((None, kdim, tn), lambda j, i: (layer, 0, j + jb))]
    args = [a, w]
    if scaled:
        in_specs.append(pl.BlockSpec((1, tn), lambda j, i: (0, j)))
        args.append(colscale.reshape(1, n).astype(F32))
    return pl.pallas_call(
        functools.partial(_mm_wres_kernel, act=act, scaled=scaled),
        grid=(n // tn, m // tm),
        in_specs=in_specs,
        out_specs=pl.BlockSpec((tm, tn), lambda j, i: (i, j)),
        out_shape=jax.ShapeDtypeStruct((m, n), out_dtype),
        scratch_shapes=[pltpu.VMEM((kdim, tn), BF16)],
        compiler_params=_cparams(("parallel", "arbitrary")),
        name="matmul_wres",
    )(*args)


def _norm_res_kernel(x_ref, y_ref, g_ref, *rest, with_next):
    y = y_ref[...].astype(F32)
    ms = jnp.mean(y * y, axis=-1, keepdims=True)
    xn = x_ref[...] + y * lax.rsqrt(ms + RMS_EPS) * g_ref[...]
    if with_next:
        g2_ref, o_ref, h_ref = rest
        o_ref[...] = xn
        ms2 = jnp.mean(xn * xn, axis=-1, keepdims=True)
        h_ref[...] = (xn * lax.rsqrt(ms2 + RMS_EPS) * g2_ref[...]).astype(h_ref.dtype)
    else:
        o_ref, = rest
        o_ref[...] = xn


def _norm_res(x, y, g, g_next=None, tm=256):
    m, d = x.shape
    with_next = g_next is not None
    row = pl.BlockSpec((tm, d), lambda i: (i, 0))
    vec = pl.BlockSpec((1, d), lambda i: (0, 0))
    in_specs = [row, row, vec]
    args = [x, y, g.reshape(1, d)]
    out_shape = jax.ShapeDtypeStruct((m, d), F32)
    out_specs = row
    if with_next:
        in_specs.append(vec)
        args.append(g_next.reshape(1, d))
        out_shape = (out_shape, jax.ShapeDtypeStruct((m, d), BF16))
        out_specs = (row, row)
    return pl.pallas_call(
        functools.partial(_norm_res_kernel, with_next=with_next),
        grid=(m // tm,),
        in_specs=in_specs,
        out_specs=out_specs,
        out_shape=out_shape,
        compiler_params=_cparams(("parallel",)),
        name="norm_res",
    )(*args)


def _t5_bucket_np(rel):
    half = NUM_BUCKETS // 2
    max_exact = half // 2
    ret = np.where(rel > 0, half, 0)
    n = np.abs(rel)
    nf = np.maximum(n, 1).astype(np.float32)
    large = max_exact + (np.log(nf / np.float32(max_exact)) / np.float32(math.log(MAX_DISTANCE / max_exact))
                         * np.float32(half - max_exact)).astype(np.int32)
    large = np.minimum(large, half - 1)
    return (ret + np.where(n < max_exact, n, large)).astype(np.int32)


def _bias_kernel(tab_ref, idx_ref, o_ref, *, col0, scale, rb, present):
    h = pl.program_id(0)
    for i, buckets in enumerate(present):
        idx = idx_ref[i * rb:(i + 1) * rb, :]
        acc = jnp.full(idx.shape, NEG_INF, F32) if -1 in buckets else None
        for b in buckets:
            if b < 0:
                continue
            val = tab_ref[b, col0 + h] * scale
            acc = jnp.full(idx.shape, val, F32) if acc is None else jnp.where(idx == b, val, acc)
        o_ref[0, i * rb:(i + 1) * rb, :] = acc


def _build_bias(rel_bias, idx, n_heads, col0, rb=128, scale=1.0):
    r, c = idx.shape
    present = tuple(tuple(int(b) for b in np.unique(idx[i:i + rb])) for i in range(0, r, rb))
    return pl.pallas_call(
        functools.partial(_bias_kernel, col0=col0, scale=scale, rb=rb, present=present),
        grid=(n_heads,),
        in_specs=[pl.BlockSpec(memory_space=pltpu.SMEM),
                  pl.BlockSpec((r, c), lambda h: (0, 0))],
        out_specs=pl.BlockSpec((1, r, c), lambda h: (h, 0, 0)),
        out_shape=jax.ShapeDtypeStruct((n_heads, r, c), F32),
        compiler_params=_cparams(("parallel",)),
        name="bias_tiles",
    )(rel_bias, jnp.asarray(idx))


def _dil_bias_idx():
    row = np.arange(DIL_QBLK)[:, None]
    col = np.arange(DIL_KWIN)[None, :]
    tiles = []
    for d in DILATIONS:
        for off in (0, -DIL_RADIUS, -2 * DIL_RADIUS):
            rel = off + col - row
            idx = _t5_bucket_np(rel * d)
            tiles.append(np.where(np.abs(rel) <= DIL_RADIUS, idx, -1))
    return np.concatenate(tiles, axis=0).astype(np.int32)


def _diff_bias_idx(seq):
    far = np.arange(DIFF_FAR, seq)
    assert np.all(_t5_bucket_np(-far) == NUM_BUCKETS // 2 - 1) and np.all(_t5_bucket_np(far) == NUM_BUCKETS - 1)
    row = np.arange(DIFF_BAND + 2 * DIFF_FAR)[:, None]
    col = np.arange(DIFF_QBLK)[None, :]
    return _t5_bucket_np(row - 2 * DIFF_FAR - col)


def _diff_key_steps(seq):
    key = np.arange(seq)[:, None]
    lane = np.arange(LANES)[None, :]
    step = key >= LANES * (lane & (DIFF_STEPS - 1))
    return np.where((lane < 2 * DIFF_STEPS) & step, 1.0, 0.0).astype(np.float32)


def _dil_kernel(q_ref, k_ref, v_ref, bias_ref, o_ref, m_scr, l_scr, n_scr, *, seq):
    for bi, d in enumerate(DILATIONS):
        sub_len = seq // d
        qblk = sub_len if sub_len == DIL_KWIN else DIL_QBLK
        nblk = sub_len // qblk
        unroll = DIL_UNROLL * DIL_QBLK // qblk
        log_d = d.bit_length() - 1
        first = bi == 0
        last = bi == len(DILATIONS) - 1

        def rows(start, size, d=d):
            if d == 1:
                return pl.ds(pl.multiple_of(start, DIL_QBLK // 2), size)
            return pl.ds(start, size, stride=d)

        def body(grp, carry, d=d, nblk=nblk, log_d=log_d, first=first, last=last, bi=bi,
                 sub_len=sub_len, rows=rows, qblk=qblk, unroll=unroll):
            qsls, scores, vbs = [], [], []
            for u in range(unroll):
                it = grp * unroll + u
                r = it & (d - 1)
                j = it >> log_d
                q0 = j * qblk
                ws = jnp.clip(q0 - DIL_RADIUS, 0, sub_len - DIL_KWIN)
                if nblk == 1:
                    bias = jnp.concatenate([bias_ref[0, bi * 3], bias_ref[0, bi * 3 + 2]], axis=0)
                else:
                    bias = bias_ref[0, bi * 3 + jnp.where(j == 0, 0, jnp.where(j == nblk - 1, 2, 1))]
                qsl = rows(r + d * q0, qblk)
                ksl = rows(r + d * ws, DIL_KWIN)
                qb = q_ref[qsl, :].astype(BF16)
                kb = k_ref[ksl, :].astype(BF16)
                s = lax.dot_general(qb, kb, (((1,), (1,)), ((), ())), preferred_element_type=F32)
                qsls.append(qsl)
                scores.append(s + bias)
                vbs.append(v_ref[ksl, :].astype(BF16))
            stats = []
            for s in scores:
                m = jnp.max(s, axis=1, keepdims=True)
                p = jnp.exp2(s - m)
                stats.append((m, jnp.sum(p, axis=1, keepdims=True), p.astype(BF16)))
            nums = [jnp.dot(p, vb, preferred_element_type=F32) for (_, _, p), vb in zip(stats, vbs)]
            shape = (qblk, LANES)
            for qsl, (m, l, _), n in zip(qsls, stats, nums):
                m = jnp.broadcast_to(m, shape)
                l = jnp.broadcast_to(l, shape)
                if not first:
                    m_old = m_scr[qsl, :]
                    m_new = jnp.maximum(m_old, m)
                    a = jnp.exp2(m_old - m_new)
                    b = jnp.exp2(m - m_new)
                    l = a * l_scr[qsl, :] + b * l
                    n = a * n_scr[qsl, :] + b * n
                    m = m_new
                if last:
                    o_ref[qsl, :] = (n / l).astype(o_ref.dtype)
                else:
                    m_scr[qsl, :] = m
                    l_scr[qsl, :] = l
                    n_scr[qsl, :] = n
            return carry

        lax.fori_loop(0, seq // (qblk * unroll), body, 0)


def _dil_attention(proj, bias_tiles, seq):
    bsz = proj.shape[0]
    nh = N_HEADS_DIL

    def col(off):
        return pl.BlockSpec((None, seq, HEAD_DIM), lambda b, h: (b, 0, off + h))

    state = pltpu.VMEM((seq, LANES), F32)
    return pl.pallas_call(
        functools.partial(_dil_kernel, seq=seq),
        grid=(bsz, nh),
        in_specs=[col(0), col(nh), col(2 * nh),
                  pl.BlockSpec((1, 3 * len(DILATIONS), DIL_QBLK, DIL_KWIN), lambda b, h: (h, 0, 0, 0))],
        out_specs=pl.BlockSpec((None, seq, HEAD_DIM), lambda b, h: (b, 0, h)),
        out_shape=jax.ShapeDtypeStruct((bsz, seq, DIL_WIDTH), BF16),
        scratch_shapes=[state, state, state],
        compiler_params=_cparams(("parallel", "parallel")),
        name="dilated_attention",
    )(proj, proj, proj, bias_tiles)


def _diff_kernel(tab_ref, lam_ref, q_ref, k_ref, ksteps_ref, vt_ref, t_ref, g_ref, o_ref, o_scr,
                 *, seq, lambda_init):
    h = pl.program_id(0)
    n_piece = seq // LANES
    pieces_per_chunk = DIFF_CHUNK // LANES
    chunk_starts = list(range(0, n_piece, pieces_per_chunk))
    width = 2 * DIFF_QBLK

    lam = (jnp.exp(jnp.sum(lam_ref[0:1, :] * lam_ref[1:2, :], axis=1, keepdims=True))
           - jnp.exp(jnp.sum(lam_ref[2:3, :] * lam_ref[3:4, :], axis=1, keepdims=True)) + lambda_init)
    c_left = tab_ref[NUM_BUCKETS // 2 - 1, N_HEADS_DIL + h] * LOG2E
    c_right = tab_ref[NUM_BUCKETS - 1, N_HEADS_DIL + h] * LOG2E
    lane = lax.broadcasted_iota(jnp.int32, (1, LANES), 1)
    lj = lane & (DIFF_STEPS - 1)
    qlane = lax.broadcasted_iota(jnp.int32, (DIFF_QBLK, LANES), 1)

    class SubBlock:
        def __init__(self, sb):
            rows = slice(sb * DIFF_QBLK, (sb + 1) * DIFF_QBLK)
            q0 = (pl.program_id(2) * DIFF_SUBS + sb) * DIFF_QBLK
            ks = jnp.clip(q0 - DIFF_FAR, 0, seq - DIFF_BAND)
            ts = ks - q0 + 2 * DIFF_FAR
            step_lo = lax.shift_right_logical(ks, LANES.bit_length() - 1)
            step_hi = step_lo + DIFF_BAND // LANES
            c = (jnp.where(lj == 0, c_left, 0.0) + jnp.where(lj == step_lo, -c_left, 0.0)
                 + jnp.where(lj == step_hi, c_right, 0.0))
            c = jnp.where(lane < 2 * DIFF_STEPS, c, 0.0)
            c_hi = c.astype(BF16).astype(F32)
            q_extra = jnp.where(lane < DIFF_STEPS, c_hi, c - c_hi).astype(BF16)
            q_extra = jnp.broadcast_to(q_extra, (DIFF_QBLK, LANES))
            q = q_ref[rows, :]
            zero = jnp.zeros_like(q)
            q_both = jnp.concatenate(
                [jnp.concatenate([jnp.where(qlane < DIFF_QK_DIM, q, zero), q_extra], axis=1),
                 jnp.concatenate([jnp.where(qlane >= DIFF_QK_DIM, q, zero), q_extra], axis=1)], axis=0)

            def piece_start(g):
                return pl.multiple_of(((step_lo + g) & (n_piece - 1)) * LANES, LANES)

            def scores(g0):
                k_c = jnp.concatenate(
                    [jnp.concatenate([k_ref[pl.ds(piece_start(g0 + i), LANES), :],
                                      ksteps_ref[pl.ds(piece_start(g0 + i), LANES), :]], axis=1)
                     for i in range(pieces_per_chunk)], axis=0)
                return lax.dot_general(k_c, q_both, (((1,), (1,)), ((), ())), preferred_element_type=F32)

            def band_bias(g0, rows_n=DIFF_CHUNK):
                tile = t_ref[0, pl.ds(pl.multiple_of(ts + g0 * LANES, LANES), rows_n), :]
                return jnp.concatenate([tile, tile], axis=1)

            def values_t(g0):
                return jnp.concatenate([vt_ref[:, pl.ds(piece_start(g0 + i), LANES)]
                                        for i in range(pieces_per_chunk)], axis=1)

            def finish(o, l):
                on = o / l
                od = on[:, :DIFF_QBLK] - lam * on[:, DIFF_QBLK:]
                ms = jnp.mean(od * od, axis=0, keepdims=True)
                y = od * lax.rsqrt(ms + SUBLN_EPS) * g_ref[...] * (1.0 - lambda_init)
                o_ref[rows, :] = y.T.astype(o_ref.dtype)

            self.scores, self.band_bias, self.values_t, self.finish = scores, band_bias, values_t, finish

    subs = [SubBlock(sb) for sb in range(DIFF_SUBS)]
    results = []
    s_next = subs[0].scores(0)
    for sb, sub in enumerate(subs):
        s = s_next
        ref = jnp.max(s[:LANES] + sub.band_bias(0, LANES), axis=0, keepdims=True)
        l8 = jnp.zeros((8, width), F32)
        o = None
        for ci, g0 in enumerate(chunk_starts):
            if ci + 1 < len(chunk_starts):
                s_next = sub.scores(chunk_starts[ci + 1])
            elif sb + 1 < len(subs):
                s_next = subs[sb + 1].scores(0)
            if g0 * LANES < DIFF_BAND:
                s = s + sub.band_bias(g0)
            p = jnp.exp2(s - ref)
            l8 = l8 + jnp.sum(p.reshape(DIFF_CHUNK // 8, 8, width), axis=0)
            o_c = jnp.dot(sub.values_t(g0), p.astype(BF16), preferred_element_type=F32)
            o = o_c if o is None else o + o_c
            s = s_next
        l = jnp.sum(l8, axis=0, keepdims=True)
        sub.finish(o, l)
        results.append((o, l))

    def all_finite(x):
        return jnp.min(jnp.where(jnp.isfinite(x), 1.0, 0.0))

    for sub, (o, l) in zip(subs, results):
        @pl.when(jnp.logical_or(jnp.minimum(all_finite(o), all_finite(l)) < 0.5, jnp.max(l) > DIFF_MAX_DENOM))
        def _(sub=sub):
            def chunk(c, carry, in_band):
                m_old, l_old = carry
                g0 = c * pieces_per_chunk
                sc = sub.scores(g0)
                if in_band:
                    sc = sc + sub.band_bias(g0)
                m_new = jnp.maximum(m_old, jnp.max(sc, axis=0, keepdims=True))
                alpha = jnp.exp2(m_old - m_new)
                pc = jnp.exp2(sc - m_new)
                o_scr[...] = alpha * o_scr[...] + jnp.dot(sub.values_t(g0), pc.astype(BF16),
                                                          preferred_element_type=F32)
                return m_new, alpha * l_old + jnp.sum(pc, axis=0, keepdims=True)

            o_scr[...] = jnp.zeros_like(o_scr)
            carry = (jnp.full((1, width), NEG_INF, F32), jnp.zeros((1, width), F32))
            n_band = DIFF_BAND // DIFF_CHUNK
            carry = lax.fori_loop(0, n_band, functools.partial(chunk, in_band=True), carry)
            _, l_slow = lax.fori_loop(n_band, len(chunk_starts), functools.partial(chunk, in_band=False), carry)
            sub.finish(o_scr[...], l_slow)


def _diff_attention(rel_bias, proj, vt, t_bias, lam_vecs, g_subln, seq, lambda_init):
    bsz = proj.shape[0]
    nh = N_HEADS_DIFF
    tq = DIFF_SUBS * DIFF_QBLK
    nq = seq // tq
    assert seq == DIFF_STEPS * LANES and 2 * DIFF_STEPS <= LANES
    assert DIFF_BAND % DIFF_CHUNK == 0 and seq % DIFF_CHUNK == 0 and seq % tq == 0
    ksteps = jnp.asarray(_diff_key_steps(seq), BF16)
    return pl.pallas_call(
        functools.partial(_diff_kernel, seq=seq, lambda_init=lambda_init),
        grid=(nh, bsz, nq),
        in_specs=[pl.BlockSpec(memory_space=pltpu.SMEM),
                  pl.BlockSpec((4, DIFF_QK_DIM), lambda h, b, i: (0, 0)),
                  pl.BlockSpec((None, tq, HEAD_DIM), lambda h, b, i: (b, i, h)),
                  pl.BlockSpec((None, seq, HEAD_DIM), lambda h, b, i: (b, 0, nh + h)),
                  pl.BlockSpec((seq, LANES), lambda h, b, i: (0, 0)),
                  pl.BlockSpec((None, None, HEAD_DIM, seq), lambda h, b, i: (b, h, 0, 0)),
                  pl.BlockSpec((1, DIFF_BAND + 2 * DIFF_FAR, DIFF_QBLK), lambda h, b, i: (h, 0, 0)),
                  pl.BlockSpec((HEAD_DIM, 1), lambda h, b, i: (0, 0))],
        out_specs=pl.BlockSpec((None, tq, HEAD_DIM), lambda h, b, i: (b, i, h)),
        out_shape=jax.ShapeDtypeStruct((bsz, seq, DIFF_WIDTH), BF16),
        scratch_shapes=[pltpu.VMEM((HEAD_DIM, 2 * DIFF_QBLK), F32)],
        compiler_params=_cparams(("parallel", "parallel", "parallel")),
        name="diff_attention",
    )(rel_bias, lam_vecs, proj, proj, ksteps, vt, t_bias, g_subln.reshape(HEAD_DIM, 1))


def kernel(x, rel_bias, g_pre_mix, w_in, lambda_q1, lambda_k1, lambda_q2, lambda_k2, g_subln, w_out,
           g_post_mix, g_pre_mlp, w_up, w_down, g_post_mlp):
    bsz, seq, d = x.shape
    m = bsz * seq
    xf = x.reshape(m, d)

    dil_tiles = _build_bias(rel_bias, _dil_bias_idx(), N_HEADS_DIL, 0, scale=LOG2E)
    dil_tiles = dil_tiles.reshape(N_HEADS_DIL, 3 * len(DILATIONS), DIL_QBLK, DIL_KWIN)
    diff_t = _build_bias(rel_bias, _diff_bias_idx(seq), N_HEADS_DIFF, N_HEADS_DIL, scale=LOG2E)

    ones = jnp.ones((DIL_WIDTH,), F32)
    scale_dil = jnp.concatenate([ones * (HEAD_DIM ** -0.5 * LOG2E), ones, ones])
    scale_diff = jnp.concatenate([ones * (DIFF_QK_DIM ** -0.5 * LOG2E), ones, ones])

    h = _rmsnorm(xf, g_pre_mix[0])
    for l in range(DEPTH):
        lambda_init = 0.8 - 0.6 * math.exp(-0.3 * l)
        proj_a = _matmul_wres(h, w_in, l, F32, colscale=scale_dil, b_cols=(0, 3 * DIL_WIDTH))
        proj_d = _matmul_wres(h, w_in, l, BF16, colscale=scale_diff, b_cols=(3 * DIL_WIDTH, 3 * DIFF_WIDTH))
        out_a = _dil_attention(proj_a.reshape(bsz, seq, 3 * DIL_WIDTH), dil_tiles, seq)
        lam_vecs = jnp.stack([lambda_q1[l], lambda_k1[l], lambda_q2[l], lambda_k2[l]]).astype(F32)
        proj_d = proj_d.reshape(bsz, seq, 3 * DIFF_WIDTH)
        vt = proj_d[:, :, 2 * DIFF_WIDTH:].reshape(bsz, seq, N_HEADS_DIFF, HEAD_DIM).transpose(0, 2, 3, 1)
        out_d = _diff_attention(rel_bias, proj_d, vt, diff_t, lam_vecs, g_subln[l], seq, lambda_init)
        y = _matmul((out_a.reshape(m, DIL_WIDTH), out_d.reshape(m, DIFF_WIDTH)), w_out, l, BF16,
                    tm=1024, tk=DIL_WIDTH)
        xf, h = _norm_res(xf, y, g_post_mix[l], g_pre_mlp[l])
        u = _matmul_wres(h, w_up, l, BF16, act="relu2")
        y = _matmul(u, w_down, l, BF16)
        if l + 1 < DEPTH:
            xf, h = _norm_res(xf, y, g_post_mlp[l], g_pre_mix[l + 1])
        else:
            xf = _norm_res(xf, y, g_post_mlp[l])
    return xf.reshape(bsz, seq, d)
```

```python
import functools
import math

import numpy as np
import jax
import jax.numpy as jnp
from jax import lax
from jax.experimental import pallas as pl
from jax.experimental.pallas import tpu as pltpu

D_MODEL = 4096
DEPTH = 2
HEAD_DIM = 128
N_HEADS_DIL = 16
N_HEADS_DIFF = 16
DIFF_QK_DIM = 64
DIL_WIDTH = N_HEADS_DIL * HEAD_DIM
DIFF_WIDTH = N_HEADS_DIFF * HEAD_DIM
DILATIONS = (16, 4, 1)
DIL_RADIUS = 64
NUM_BUCKETS = 32
MAX_DISTANCE = 1024
RMS_EPS = 1e-6
SUBLN_EPS = 1e-5
NEG_INF = -1e30

LANES = 128
DIL_QBLK = 128
DIL_KWIN = 256
DIL_UNROLL = 8
DIFF_QBLK = 512
DIFF_SUBS = 2
DIFF_STEPS = 32
DIFF_FAR = 768
DIFF_BAND = 2 * DIFF_FAR + DIFF_QBLK
DIFF_CHUNK = 512
DIFF_MAX_DENOM = 2.0 ** 80
LOG2E = 1.4426950408889634
VMEM_LIMIT = 61 * 1024 * 1024

F32 = jnp.float32
BF16 = jnp.bfloat16


def _cparams(sem, flags=None):
    return pltpu.CompilerParams(dimension_semantics=sem, vmem_limit_bytes=VMEM_LIMIT, flags=flags)


def _rmsnorm_kernel(x_ref, g_ref, o_ref):
    x = x_ref[...]
    ms = jnp.mean(x * x, axis=-1, keepdims=True)
    o_ref[...] = (x * lax.rsqrt(ms + RMS_EPS) * g_ref[...]).astype(o_ref.dtype)


def _rmsnorm(x, g, tm=256):
    m, d = x.shape
    return pl.pallas_call(
        _rmsnorm_kernel,
        grid=(m // tm,),
        in_specs=[pl.BlockSpec((tm, d), lambda i: (i, 0)),
                  pl.BlockSpec((1, d), lambda i: (0, 0))],
        out_specs=pl.BlockSpec((tm, d), lambda i: (i, 0)),
        out_shape=jax.ShapeDtypeStruct((m, d), BF16),
        compiler_params=_cparams(("parallel",)),
        name="rmsnorm",
    )(x, g.reshape(1, d))


def _mm_epilogue(r, s_ref, act, dtype):
    if s_ref is not None:
        r = r * s_ref[...]
    if act == "relu2":
        r = jnp.square(jnp.maximum(r, 0.0))
    return r.astype(dtype)


def _mm_kernel(*refs, nk, act, scaled, split_a):
    n_a = 2 if split_a else 1
    a_refs = refs[:n_a]
    b_ref = refs[n_a]
    s_ref = refs[n_a + 1] if scaled else None
    o_ref = refs[n_a + 1 + scaled]

    acc_ref, = refs[n_a + 2 + scaled:]
    k = pl.program_id(2)

    def partial_product(a_ref):
        return jnp.dot(a_ref[...], b_ref[...].astype(BF16), preferred_element_type=F32)

    @pl.when(k == 0)
    def _():
        acc_ref[...] = partial_product(a_refs[0])

    @pl.when(jnp.logical_and(k > 0, k < nk - 1))
    def _():
        acc_ref[...] += partial_product(a_refs[0])

    @pl.when(k == nk - 1)
    def _():
        o_ref[...] = _mm_epilogue(acc_ref[...] + partial_product(a_refs[-1]), s_ref, act, o_ref.dtype)


def _matmul(a, w, layer, out_dtype, colscale=None, act=None, b_cols=None, tm=2048, tn=1024, tk=1024):
    split_a = isinstance(a, tuple)
    a_list = list(a) if split_a else [a]
    m = a_list[0].shape[0]
    kdim = sum(x.shape[1] for x in a_list)
    c0, n = b_cols if b_cols is not None else (0, w.shape[2])
    jb = c0 // tn
    assert c0 % tn == 0 and n % tn == 0 and m % tm == 0 and kdim % tk == 0
    nk = kdim // tk
    assert nk >= 2
    scaled = colscale is not None
    if split_a:
        assert nk == 2 and all(x.shape == (m, tk) for x in a_list)
        in_specs = [pl.BlockSpec((tm, tk), lambda i, j, k: (i, 0), pipeline_mode=pl.Buffered(1))
                    for _ in a_list]
    else:
        in_specs = [pl.BlockSpec((tm, tk), lambda i, j, k: (i, k))]
    in_specs.append(pl.BlockSpec((None, tk, tn), lambda i, j, k: (layer, k, j + jb)))
    args = a_list + [w]
    if scaled:
        in_specs.append(pl.BlockSpec((1, tn), lambda i, j, k: (0, j)))
        args.append(colscale.reshape(1, n).astype(F32))
    return pl.pallas_call(
        functools.partial(_mm_kernel, nk=nk, act=act, scaled=scaled, split_a=split_a),
        grid=(m // tm, n // tn, nk),
        in_specs=in_specs,
        out_specs=pl.BlockSpec((tm, tn), lambda i, j, k: (i, j)),
        out_shape=jax.ShapeDtypeStruct((m, n), out_dtype),
        scratch_shapes=[pltpu.VMEM((tm, tn), F32)],
        compiler_params=_cparams(("parallel", "parallel", "arbitrary")),
        name="matmul",
    )(*args)


def _mm_wres_kernel(*refs, act, scaled):
    if scaled:
        a_ref, b_ref, s_ref, o_ref, b16_ref = refs
    else:
        a_ref, b_ref, o_ref, b16_ref = refs
        s_ref = None

    @pl.when(pl.program_id(1) == 0)
    def _():
        b16_ref[...] = b_ref[...].astype(BF16)

    r = jnp.dot(a_ref[...], b16_ref[...], preferred_element_type=F32)
    o_ref[...] = _mm_epilogue(r, s_ref, act, o_ref.dtype)


def _matmul_wres(a, w, layer, out_dtype, colscale=None, act=None, b_cols=None, tm=512, tn=1024):
    m, kdim = a.shape
    c0, n = b_cols if b_cols is not None else (0, w.shape[2])
    jb = c0 // tn
    assert c0 % tn == 0 and n % tn == 0 and m % tm == 0
    scaled = colscale is not None
    in_specs = [pl.BlockSpec((tm, kdim), lambda j, i: (i, 0)),
                pl.BlockSpec((None, kdim, tn), lambda j, i: (layer, 0, j + jb))]
    args = [a, w]
    if scaled:
        in_specs.append(pl.BlockSpec((1, tn), lambda j, i: (0, j)))
        args.append(colscale.reshape(1, n).astype(F32))
    return pl.pallas_call(
        functools.partial(_mm_wres_kernel, act=act, scaled=scaled),
        grid=(n // tn, m // tm),
        in_specs=in_specs,
        out_specs=pl.BlockSpec((tm, tn), lambda j, i: (i, j)),
        out_shape=jax.ShapeDtypeStruct((m, n), out_dtype),
        scratch_shapes=[pltpu.VMEM((kdim, tn), BF16)],
        compiler_params=_cparams(("parallel", "arbitrary")),
        name="matmul_wres",
    )(*args)


def _norm_res_kernel(x_ref, y_ref, g_ref, *rest, with_next):
    y = y_ref[...].astype(F32)
    ms = jnp.mean(y * y, axis=-1, keepdims=True)
    xn = x_ref[...] + y * lax.rsqrt(ms + RMS_EPS) * g_ref[...]
    if with_next:
        g2_ref, o_ref, h_ref = rest
        o_ref[...] = xn
        ms2 = jnp.mean(xn * xn, axis=-1, keepdims=True)
        h_ref[...] = (xn * lax.rsqrt(ms2 + RMS_EPS) * g2_ref[...]).astype(h_ref.dtype)
    else:
        o_ref, = rest
        o_ref[...] = xn


def _norm_res(x, y, g, g_next=None, tm=256):
    m, d = x.shape
    with_next = g_next is not None
    row = pl.BlockSpec((tm, d), lambda i: (i, 0))
    vec = pl.BlockSpec((1, d), lambda i: (0, 0))
    in_specs = [row, row, vec]
    args = [x, y, g.reshape(1, d)]
    out_shape = jax.ShapeDtypeStruct((m, d), F32)
    out_specs = row
    if with_next:
        in_specs.append(vec)
        args.append(g_next.reshape(1, d))
        out_shape = (out_shape, jax.ShapeDtypeStruct((m, d), BF16))
        out_specs = (row, row)
    return pl.pallas_call(
        functools.partial(_norm_res_kernel, with_next=with_next),
        grid=(m // tm,),
        in_specs=in_specs,
        out_specs=out_specs,
        out_shape=out_shape,
        compiler_params=_cparams(("parallel",)),
        name="norm_res",
    )(*args)


def _t5_bucket_np(rel):
    half = NUM_BUCKETS // 2
    max_exact = half // 2
    ret = np.where(rel > 0, half, 0)
    n = np.abs(rel)
    nf = np.maximum(n, 1).astype(np.float32)
    large = max_exact + (np.log(nf / np.float32(max_exact)) / np.float32(math.log(MAX_DISTANCE / max_exact))
                         * np.float32(half - max_exact)).astype(np.int32)
    large = np.minimum(large, half - 1)
    return (ret + np.where(n < max_exact, n, large)).astype(np.int32)


def _bias_kernel(tab_ref, idx_ref, o_ref, *, col0, scale, rb, present):
    h = pl.program_id(0)
    for i, buckets in enumerate(present):
        idx = idx_ref[i * rb:(i + 1) * rb, :]
        acc = jnp.full(idx.shape, NEG_INF, F32) if -1 in buckets else None
        for b in buckets:
            if b < 0:
                continue
            val = tab_ref[b, col0 + h] * scale
            acc = jnp.full(idx.shape, val, F32) if acc is None else jnp.where(idx == b, val, acc)
        o_ref[0, i * rb:(i + 1) * rb, :] = acc


def _build_bias(rel_bias, idx, n_heads, col0, rb=128, scale=1.0):
    r, c = idx.shape
    present = tuple(tuple(int(b) for b in np.unique(idx[i:i + rb])) for i in range(0, r, rb))
    return pl.pallas_call(
        functools.partial(_bias_kernel, col0=col0, scale=scale, rb=rb, present=present),
        grid=(n_heads,),
        in_specs=[pl.BlockSpec(memory_space=pltpu.SMEM),
                  pl.BlockSpec((r, c), lambda h: (0, 0))],
        out_specs=pl.BlockSpec((1, r, c), lambda h: (h, 0, 0)),
        out_shape=jax.ShapeDtypeStruct((n_heads, r, c), F32),
        compiler_params=_cparams(("parallel",)),
        name="bias_tiles",
    )(rel_bias, jnp.asarray(idx))


def _dil_bias_idx():
    row = np.arange(DIL_QBLK)[:, None]
    col = np.arange(DIL_KWIN)[None, :]
    tiles = []
    for d in DILATIONS:
        for off in (0, -DIL_RADIUS, -2 * DIL_RADIUS):
            rel = off + col - row
            idx = _t5_bucket_np(rel * d)
            tiles.append(np.where(np.abs(rel) <= DIL_RADIUS, idx, -1))
    return np.concatenate(tiles, axis=0).astype(np.int32)


def _diff_bias_idx(seq):
    far = np.arange(DIFF_FAR, seq)
    assert np.all(_t5_bucket_np(-far) == NUM_BUCKETS // 2 - 1) and np.all(_t5_bucket_np(far) == NUM_BUCKETS - 1)
    row = np.arange(DIFF_BAND + 2 * DIFF_FAR)[:, None]
    col = np.arange(DIFF_QBLK)[None, :]
    return _t5_bucket_np(row - 2 * DIFF_FAR - col)


def _diff_key_steps(seq):
    key = np.arange(seq)[:, None]
    lane = np.arange(LANES)[None, :]
    step = key >= LANES * (lane & (DIFF_STEPS - 1))
    return np.where((lane < 2 * DIFF_STEPS) & step, 1.0, 0.0).astype(np.float32)


def _dil_kernel(q_ref, k_ref, v_ref, bias_ref, o_ref, m_scr, l_scr, n_scr, *, seq):
    for bi, d in enumerate(DILATIONS):
        sub_len = seq // d
        qblk = sub_len if sub_len == DIL_KWIN else DIL_QBLK
        nblk = sub_len // qblk
        unroll = DIL_UNROLL * DIL_QBLK // qblk
        log_d = d.bit_length() - 1
        first = bi == 0
        last = bi == len(DILATIONS) - 1

        def rows(start, size, d=d):
            if d == 1:
                return pl.ds(pl.multiple_of(start, DIL_QBLK // 2), size)
            return pl.ds(start, size, stride=d)

        def body(grp, carry, d=d, nblk=nblk, log_d=log_d, first=first, last=last, bi=bi,
                 sub_len=sub_len, rows=rows, qblk=qblk, unroll=unroll):
            qsls, scores, vbs = [], [], []
            for u in range(unroll):
                it = grp * unroll + u
                r = it & (d - 1)
                j = it >> log_d
                q0 = j * qblk
                ws = jnp.clip(q0 - DIL_RADIUS, 0, sub_len - DIL_KWIN)
                if nblk == 1:
                    bias = jnp.concatenate([bias_ref[0, bi * 3], bias_ref[0, bi * 3 + 2]], axis=0)
                else:
                    bias = bias_ref[0, bi * 3 + jnp.where(j == 0, 0, jnp.where(j == nblk - 1, 2, 1))]
                qsl = rows(r + d * q0, qblk)
                ksl = rows(r + d * ws, DIL_KWIN)
                qb = q_ref[qsl, :].astype(BF16)
                kb = k_ref[ksl, :].astype(BF16)
                s = lax.dot_general(qb, kb, (((1,), (1,)), ((), ())), preferred_element_type=F32)
                qsls.append(qsl)
                scores.append(s + bias)
                vbs.append(v_ref[ksl, :].astype(BF16))
            stats = []
            for s in scores:
                m = jnp.max(s, axis=1, keepdims=True)
                p = jnp.exp2(s - m)
                stats.append((m, jnp.sum(p, axis=1, keepdims=True), p.astype(BF16)))
            nums = [jnp.dot(p, vb, preferred_element_type=F32) for (_, _, p), vb in zip(stats, vbs)]
            shape = (qblk, LANES)
            for qsl, (m, l, _), n in zip(qsls, stats, nums):
                m = jnp.broadcast_to(m, shape)
                l = jnp.broadcast_to(l, shape)
                if not first:
                    m_old = m_scr[qsl, :]
                    m_new = jnp.maximum(m_old, m)
                    a = jnp.exp2(m_old - m_new)
                    b = jnp.exp2(m - m_new)
                    l = a * l_scr[qsl, :] + b * l
                    n = a * n_scr[qsl, :] + b * n
                    m = m_new
                if last:
                    o_ref[qsl, :] = (n / l).astype(o_ref.dtype)
                else:
                    m_scr[qsl, :] = m
                    l_scr[qsl, :] = l
                    n_scr[qsl, :] = n
            return carry

        lax.fori_loop(0, seq // (qblk * unroll), body, 0)


def _dil_attention(proj, bias_tiles, seq):
    bsz = proj.shape[0]
    nh = N_HEADS_DIL

    def col(off):
        return pl.BlockSpec((None, seq, HEAD_DIM), lambda b, h: (b, 0, off + h))

    state = pltpu.VMEM((seq, LANES), F32)
    return pl.pallas_call(
        functools.partial(_dil_kernel, seq=seq),
        grid=(bsz, nh),
        in_specs=[col(0), col(nh), col(2 * nh),
                  pl.BlockSpec((1, 3 * len(DILATIONS), DIL_QBLK, DIL_KWIN), lambda b, h: (h, 0, 0, 0))],
        out_specs=pl.BlockSpec((None, seq, HEAD_DIM), lambda b, h: (b, 0, h)),
        out_shape=jax.ShapeDtypeStruct((bsz, seq, DIL_WIDTH), BF16),
        scratch_shapes=[state, state, state],
        compiler_params=_cparams(("parallel", "parallel")),
        name="dilated_attention",
    )(proj, proj, proj, bias_tiles)


def _diff_kernel(tab_ref, lam_ref, q_ref, k_ref, ksteps_ref, vt_ref, t_ref, g_ref, o_ref, o_scr,
                 *, seq, lambda_init):
    h = pl.program_id(0)
    n_piece = seq // LANES
    pieces_per_chunk = DIFF_CHUNK // LANES
    chunk_starts = list(range(0, n_piece, pieces_per_chunk))
    width = 2 * DIFF_QBLK

    lam = (jnp.exp(jnp.sum(lam_ref[0:1, :] * lam_ref[1:2, :], axis=1, keepdims=True))
           - jnp.exp(jnp.sum(lam_ref[2:3, :] * lam_ref[3:4, :], axis=1, keepdims=True)) + lambda_init)
    c_left = tab_ref[NUM_BUCKETS // 2 - 1, N_HEADS_DIL + h] * LOG2E
    c_right = tab_ref[NUM_BUCKETS - 1, N_HEADS_DIL + h] * LOG2E
    lane = lax.broadcasted_iota(jnp.int32, (1, LANES), 1)
    lj = lane & (DIFF_STEPS - 1)
    qlane = lax.broadcasted_iota(jnp.int32, (DIFF_QBLK, LANES), 1)

    class SubBlock:
        def __init__(self, sb):
            rows = slice(sb * DIFF_QBLK, (sb + 1) * DIFF_QBLK)
            q0 = (pl.program_id(2) * DIFF_SUBS + sb) * DIFF_QBLK
            ks = jnp.clip(q0 - DIFF_FAR, 0, seq - DIFF_BAND)
            ts = ks - q0 + 2 * DIFF_FAR
            step_lo = lax.shift_right_logical(ks, LANES.bit_length() - 1)
            step_hi = step_lo + DIFF_BAND // LANES
            c = (jnp.where(lj == 0, c_left, 0.0) + jnp.where(lj == step_lo, -c_left, 0.0)
                 + jnp.where(lj == step_hi, c_right, 0.0))
            c = jnp.where(lane < 2 * DIFF_STEPS, c, 0.0)
            c_hi = c.astype(BF16).astype(F32)
            q_extra = jnp.where(lane < DIFF_STEPS, c_hi, c - c_hi).astype(BF16)
            q_extra = jnp.broadcast_to(q_extra, (DIFF_QBLK, LANES))
            q = q_ref[rows, :]
            zero = jnp.zeros_like(q)
            q_both = jnp.concatenate(
                [jnp.concatenate([jnp.where(qlane < DIFF_QK_DIM, q, zero), q_extra], axis=1),
                 jnp.concatenate([jnp.where(qlane >= DIFF_QK_DIM, q, zero), q_extra], axis=1)], axis=0)

            def piece_start(g):
                return pl.multiple_of(((step_lo + g) & (n_piece - 1)) * LANES, LANES)

            def scores(g0):
                k_c = jnp.concatenate(
                    [jnp.concatenate([k_ref[pl.ds(piece_start(g0 + i), LANES), :],
                                      ksteps_ref[pl.ds(piece_start(g0 + i), LANES), :]], axis=1)
                     for i in range(pieces_per_chunk)], axis=0)
                return lax.dot_general(k_c, q_both, (((1,), (1,)), ((), ())), preferred_element_type=F32)

            def band_bias(g0, rows_n=DIFF_CHUNK):
                tile = t_ref[0, pl.ds(pl.multiple_of(ts + g0 * LANES, LANES), rows_n), :]
                return jnp.concatenate([tile, tile], axis=1)

            def values_t(g0):
                return jnp.concatenate([vt_ref[:, pl.ds(piece_start(g0 + i), LANES)]
                                        for i in range(pieces_per_chunk)], axis=1)

            def finish(o, l):
                on = o / l
                od = on[:, :DIFF_QBLK] - lam * on[:, DIFF_QBLK:]
                ms = jnp.mean(od * od, axis=0, keepdims=True)
                y = od * lax.rsqrt(ms + SUBLN_EPS) * g_ref[...] * (1.0 - lambda_init)
                o_ref[rows, :] = y.T.astype(o_ref.dtype)

            self.scores, self.band_bias, self.values_t, self.finish = scores, band_bias, values_t, finish

    subs = [SubBlock(sb) for sb in range(DIFF_SUBS)]
    results = []
    s_next = subs[0].scores(0)
    for sb, sub in enumerate(subs):
        s = s_next
        ref = jnp.max(s[:LANES] + sub.band_bias(0, LANES), axis=0, keepdims=True)
        l8 = jnp.zeros((8, width), F32)
        o = None
        for ci, g0 in enumerate(chunk_starts):
            if ci + 1 < len(chunk_starts):
                s_next = sub.scores(chunk_starts[ci + 1])
            elif sb + 1 < len(subs):
                s_next = subs[sb + 1].scores(0)
            if g0 * LANES < DIFF_BAND:
                s = s + sub.band_bias(g0)
            p = jnp.exp2(s - ref)
            l8 = l8 + jnp.sum(p.reshape(DIFF_CHUNK // 8, 8, width), axis=0)
            o_c = jnp.dot(sub.values_t(g0), p.astype(BF16), preferred_element_type=F32)
            o = o_c if o is None else o + o_c
            s = s_next
        l = jnp.sum(l8, axis=0, keepdims=True)
        sub.finish(o, l)
        results.append((o, l))

    def all_finite(x):
        return jnp.min(jnp.where(jnp.isfinite(x), 1.0, 0.0))

    for sub, (o, l) in zip(subs, results):
        @pl.when(jnp.logical_or(jnp.minimum(all_finite(o), all_finite(l)) < 0.5, jnp.max(l) > DIFF_MAX_DENOM))
        def _(sub=sub):
            def chunk(c, carry, in_band):
                m_old, l_old = carry
                g0 = c * pieces_per_chunk
                sc = sub.scores(g0)
                if in_band:
                    sc = sc + sub.band_bias(g0)
                m_new = jnp.maximum(m_old, jnp.max(sc, axis=0, keepdims=True))
                alpha = jnp.exp2(m_old - m_new)
                pc = jnp.exp2(sc - m_new)
                o_scr[...] = alpha * o_scr[...] + jnp.dot(sub.values_t(g0), pc.astype(BF16),
                                                          preferred_element_type=F32)
                return m_new, alpha * l_old + jnp.sum(pc, axis=0, keepdims=True)

            o_scr[...] = jnp.zeros_like(o_scr)
            carry = (jnp.full((1, width), NEG_INF, F32), jnp.zeros((1, width), F32))
            n_band = DIFF_BAND // DIFF_CHUNK
            carry = lax.fori_loop(0, n_band, functools.partial(chunk, in_band=True), carry)
            _, l_slow = lax.fori_loop(n_band, len(chunk_starts), functools.partial(chunk, in_band=False), carry)
            sub.finish(o_scr[...], l_slow)


def _diff_attention(rel_bias, proj, vt, t_bias, lam_vecs, g_subln, seq, lambda_init):
    bsz = proj.shape[0]
    nh = N_HEADS_DIFF
    tq = DIFF_SUBS * DIFF_QBLK
    nq = seq // tq
    assert seq == DIFF_STEPS * LANES and 2 * DIFF_STEPS <= LANES
    assert DIFF_BAND % DIFF_CHUNK == 0 and seq % DIFF_CHUNK == 0 and seq % tq == 0
    ksteps = jnp.asarray(_diff_key_steps(seq), BF16)
    return pl.pallas_call(
        functools.partial(_diff_kernel, seq=seq, lambda_init=lambda_init),
        grid=(nh, bsz, nq),
        in_specs=[pl.BlockSpec(memory_space=pltpu.SMEM),
                  pl.BlockSpec((4, DIFF_QK_DIM), lambda h, b, i: (0, 0)),
                  pl.BlockSpec((None, tq, HEAD_DIM), lambda h, b, i: (b, i, h)),
                  pl.BlockSpec((None, seq, HEAD_DIM), lambda h, b, i: (b, 0, nh + h)),
                  pl.BlockSpec((seq, LANES), lambda h, b, i: (0, 0)),
                  pl.BlockSpec((None, None, HEAD_DIM, seq), lambda h, b, i: (b, h, 0, 0)),
                  pl.BlockSpec((1, DIFF_BAND + 2 * DIFF_FAR, DIFF_QBLK), lambda h, b, i: (h, 0, 0)),
                  pl.BlockSpec((HEAD_DIM, 1), lambda h, b, i: (0, 0))],
        out_specs=pl.BlockSpec((None, tq, HEAD_DIM), lambda h, b, i: (b, i, h)),
        out_shape=jax.ShapeDtypeStruct((bsz, seq, DIFF_WIDTH), BF16),
        scratch_shapes=[pltpu.VMEM((HEAD_DIM, 2 * DIFF_QBLK), F32)],
        compiler_params=_cparams(("parallel", "parallel", "parallel")),
        name="diff_attention",
    )(rel_bias, lam_vecs, proj, proj, ksteps, vt, t_bias, g_subln.reshape(HEAD_DIM, 1))


def kernel(x, rel_bias, g_pre_mix, w_in, lambda_q1, lambda_k1, lambda_q2, lambda_k2, g_subln, w_out,
           g_post_mix, g_pre_mlp, w_up, w_down, g_post_mlp):
    bsz, seq, d = x.shape
    m = bsz * seq
    xf = x.reshape(m, d)

    dil_tiles = _build_bias(rel_bias, _dil_bias_idx(), N_HEADS_DIL, 0, scale=LOG2E)
    dil_tiles = dil_tiles.reshape(N_HEADS_DIL, 3 * len(DILATIONS), DIL_QBLK, DIL_KWIN)
    diff_t = _build_bias(rel_bias, _diff_bias_idx(seq), N_HEADS_DIFF, N_HEADS_DIL, scale=LOG2E)

    ones = jnp.ones((DIL_WIDTH,), F32)
    scale_dil = jnp.concatenate([ones * (HEAD_DIM ** -0.5 * LOG2E), ones, ones])
    scale_diff = jnp.concatenate([ones * (DIFF_QK_DIM ** -0.5 * LOG2E), ones, ones])

    h = _rmsnorm(xf, g_pre_mix[0])
    for l in range(DEPTH):
        lambda_init = 0.8 - 0.6 * math.exp(-0.3 * l)
        proj_a = _matmul_wres(h, w_in, l, F32, colscale=scale_dil, b_cols=(0, 3 * DIL_WIDTH))
        proj_d = _matmul_wres(h, w_in, l, BF16, colscale=scale_diff, b_cols=(3 * DIL_WIDTH, 3 * DIFF_WIDTH))
        out_a = _dil_attention(proj_a.reshape(bsz, seq, 3 * DIL_WIDTH), dil_tiles, seq)
        lam_vecs = jnp.stack([lambda_q1[l], lambda_k1[l], lambda_q2[l], lambda_k2[l]]).astype(F32)
        proj_d = proj_d.reshape(bsz, seq, 3 * DIFF_WIDTH)
        vt = proj_d[:, :, 2 * DIFF_WIDTH:].reshape(bsz, seq, N_HEADS_DIFF, HEAD_DIM).transpose(0, 2, 3, 1)
        out_d = _diff_attention(rel_bias, proj_d, vt, diff_t, lam_vecs, g_subln[l], seq, lambda_init)
        y = _matmul((out_a.reshape(m, DIL_WIDTH), out_d.reshape(m, DIFF_WIDTH)), w_out, l, BF16,
                    tm=1024, tk=DIL_WIDTH)
        xf, h = _norm_res(xf, y, g_post_mix[l], g_pre_mlp[l])
        u = _matmul_wres(h, w_up, l, BF16, act="relu2")
        y = _matmul(u, w_down, l, BF16, tk=2048)
        if l + 1 < DEPTH:
            xf, h = _norm_res(xf, y, g_post_mlp[l], g_pre_mix[l + 1])
        else:
            xf = _norm_res(xf, y, g_post_mlp[l])
    return xf.reshape(bsz, seq, d)
```

```python
import functools
import math

import numpy as np
import jax
import jax.numpy as jnp
from jax import lax
from jax.experimental import pallas as pl
from jax.experimental.pallas import tpu as pltpu

D_MODEL = 4096
DEPTH = 2
HEAD_DIM = 128
N_HEADS_DIL = 16
N_HEADS_DIFF = 16
DIFF_QK_DIM = 64
DIL_WIDTH = N_HEADS_DIL * HEAD_DIM
DIFF_WIDTH = N_HEADS_DIFF * HEAD_DIM
DILATIONS = (16, 4, 1)
DIL_RADIUS = 64
NUM_BUCKETS = 32
MAX_DISTANCE = 1024
RMS_EPS = 1e-6
SUBLN_EPS = 1e-5
NEG_INF = -1e30

LANES = 128
DIL_QBLK = 128
DIL_KWIN = 256
DIL_UNROLL = 8
DIFF_QBLK = 512
DIFF_SUBS = 2
DIFF_STEPS = 32
DIFF_FAR = 768
DIFF_BAND = 2 * DIFF_FAR + DIFF_QBLK
DIFF_CHUNK = 512
DIFF_MAX_DENOM = 2.0 ** 80
LOG2E = 1.4426950408889634
VMEM_LIMIT = 61 * 1024 * 1024

F32 = jnp.float32
BF16 = jnp.bfloat16


def _cparams(sem, flags=None):
    return pltpu.CompilerParams(dimension_semantics=sem, vmem_limit_bytes=VMEM_LIMIT, flags=flags)


def _rmsnorm_kernel(x_ref, g_ref, o_ref):
    x = x_ref[...]
    ms = jnp.mean(x * x, axis=-1, keepdims=True)
    o_ref[...] = (x * lax.rsqrt(ms + RMS_EPS) * g_ref[...]).astype(o_ref.dtype)


def _rmsnorm(x, g, tm=256):
    m, d = x.shape
    return pl.pallas_call(
        _rmsnorm_kernel,
        grid=(m // tm,),
        in_specs=[pl.BlockSpec((tm, d), lambda i: (i, 0)),
                  pl.BlockSpec((1, d), lambda i: (0, 0))],
        out_specs=pl.BlockSpec((tm, d), lambda i: (i, 0)),
        out_shape=jax.ShapeDtypeStruct((m, d), BF16),
        compiler_params=_cparams(("parallel",)),
        name="rmsnorm",
    )(x, g.reshape(1, d))


def _mm_epilogue(r, s_ref, act, dtype):
    if s_ref is not None:
        r = r * s_ref[...]
    if act == "relu2":
        r = jnp.square(jnp.maximum(r, 0.0))
    return r.astype(dtype)


def _mm_kernel(*refs, nk, act, scaled, split_a):
    n_a = 2 if split_a else 1
    a_refs = refs[:n_a]
    b_ref = refs[n_a]
    s_ref = refs[n_a + 1] if scaled else None
    o_ref = refs[n_a + 1 + scaled]

    acc_ref, = refs[n_a + 2 + scaled:]
    k = pl.program_id(2)

    def partial_product(a_ref):
        return jnp.dot(a_ref[...], b_ref[...].astype(BF16), preferred_element_type=F32)

    @pl.when(k == 0)
    def _():
        acc_ref[...] = partial_product(a_refs[0])

    @pl.when(jnp.logical_and(k > 0, k < nk - 1))
    def _():
        acc_ref[...] += partial_product(a_refs[0])

    @pl.when(k == nk - 1)
    def _():
        o_ref[...] = _mm_epilogue(acc_ref[...] + partial_product(a_refs[-1]), s_ref, act, o_ref.dtype)


def _matmul(a, w, layer, out_dtype, colscale=None, act=None, b_cols=None, tm=2048, tn=1024, tk=1024):
    split_a = isinstance(a, tuple)
    a_list = list(a) if split_a else [a]
    m = a_list[0].shape[0]
    kdim = sum(x.shape[1] for x in a_list)
    c0, n = b_cols if b_cols is not None else (0, w.shape[2])
    jb = c0 // tn
    assert c0 % tn == 0 and n % tn == 0 and m % tm == 0 and kdim % tk == 0
    nk = kdim // tk
    assert nk >= 2
    scaled = colscale is not None
    if split_a:
        assert nk == 2 and all(x.shape == (m, tk) for x in a_list)
        in_specs = [pl.BlockSpec((tm, tk), lambda i, j, k: (i, 0), pipeline_mode=pl.Buffered(1))
                    for _ in a_list]
    else:
        in_specs = [pl.BlockSpec((tm, tk), lambda i, j, k: (i, k))]
    in_specs.append(pl.BlockSpec((None, tk, tn), lambda i, j, k: (layer, k, j + jb)))
    args = a_list + [w]
    if scaled:
        in_specs.append(pl.BlockSpec((1, tn), lambda i, j, k: (0, j)))
        args.append(colscale.reshape(1, n).astype(F32))
    return pl.pallas_call(
        functools.partial(_mm_kernel, nk=nk, act=act, scaled=scaled, split_a=split_a),
        grid=(m // tm, n // tn, nk),
        in_specs=in_specs,
        out_specs=pl.BlockSpec((tm, tn), lambda i, j, k: (i, j)),
        out_shape=jax.ShapeDtypeStruct((m, n), out_dtype),
        scratch_shapes=[pltpu.VMEM((tm, tn), F32)],
        compiler_params=_cparams(("parallel", "parallel", "arbitrary")),
        name="matmul",
    )(*args)


def _mm_wres_kernel(*refs, act, scaled):
    if scaled:
        a_ref, b_ref, s_ref, o_ref, b16_ref = refs
    else:
        a_ref, b_ref, o_ref, b16_ref = refs
        s_ref = None

    @pl.when(pl.program_id(1) == 0)
    def _():
        b16_ref[...] = b_ref[...].astype(BF16)

    r = jnp.dot(a_ref[...], b16_ref[...], preferred_element_type=F32)
    o_ref[...] = _mm_epilogue(r, s_ref, act, o_ref.dtype)


def _matmul_wres(a, w, layer, out_dtype, colscale=None, act=None, b_cols=None, tm=512, tn=1024):
    m, kdim = a.shape
    c0, n = b_cols if b_cols is not None else (0, w.shape[2])
    jb = c0 // tn
    assert c0 % tn == 0 and n % tn == 0 and m % tm == 0
    scaled = colscale is not None
    in_specs = [pl.BlockSpec((tm, kdim), lambda j, i: (i, 0)),
                pl.BlockSpec((None, kdim, tn), lambda j, i: (layer, 0, j + jb))]
    args = [a, w]
    if scaled:
        in_specs.append(pl.BlockSpec((1, tn), lambda j, i: (0, j)))
        args.append(colscale.reshape(1, n).astype(F32))
    return pl.pallas_call(
        functools.partial(_mm_wres_kernel, act=act, scaled=scaled),
        grid=(n // tn, m // tm),
        in_specs=in_specs,
        out_specs=pl.BlockSpec((tm, tn), lambda j, i: (i, j)),
        out_shape=jax.ShapeDtypeStruct((m, n), out_dtype),
        scratch_shapes=[pltpu.VMEM((kdim, tn), BF16)],
        compiler_params=_cparams(("parallel", "arbitrary")),
        name="matmul_wres",
    )(*args)


def _norm_res_kernel(x_ref, y_ref, g_ref, *rest, with_next):
    y = y_ref[...].astype(F32)
    ms = jnp.mean(y * y, axis=-1, keepdims=True)
    xn = x_ref[...] + y * lax.rsqrt(ms + RMS_EPS) * g_ref[...]
    if with_next:
        g2_ref, o_ref, h_ref = rest
        o_ref[...] = xn
        ms2 = jnp.mean(xn * xn, axis=-1, keepdims=True)
        h_ref[...] = (xn * lax.rsqrt(ms2 + RMS_EPS) * g2_ref[...]).astype(h_ref.dtype)
    else:
        o_ref, = rest
        o_ref[...] = xn


def _norm_res(x, y, g, g_next=None, tm=256):
    m, d = x.shape
    with_next = g_next is not None
    row = pl.BlockSpec((tm, d), lambda i: (i, 0))
    vec = pl.BlockSpec((1, d), lambda i: (0, 0))
    in_specs = [row, row, vec]
    args = [x, y, g.reshape(1, d)]
    out_shape = jax.ShapeDtypeStruct((m, d), F32)
    out_specs = row
    if with_next:
        in_specs.append(vec)
        args.append(g_next.reshape(1, d))
        out_shape = (out_shape, jax.ShapeDtypeStruct((m, d), BF16))
        out_specs = (row, row)
    return pl.pallas_call(
        functools.partial(_norm_res_kernel, with_next=with_next),
        grid=(m // tm,),
        in_specs=in_specs,
        out_specs=out_specs,
        out_shape=out_shape,
        compiler_params=_cparams(("parallel",)),
        name="norm_res",
    )(*args)


def _t5_bucket_np(rel):
    half = NUM_BUCKETS // 2
    max_exact = half // 2
    ret = np.where(rel > 0, half, 0)
    n = np.abs(rel)
    nf = np.maximum(n, 1).astype(np.float32)
    large = max_exact + (np.log(nf / np.float32(max_exact)) / np.float32(math.log(MAX_DISTANCE / max_exact))
                         * np.float32(half - max_exact)).astype(np.int32)
    large = np.minimum(large, half - 1)
    return (ret + np.where(n < max_exact, n, large)).astype(np.int32)


def _bias_kernel(tab_ref, idx_ref, o_ref, *, col0, scale, rb, present):
    h = pl.program_id(0)
    for i, buckets in enumerate(present):
        idx = idx_ref[i * rb:(i + 1) * rb, :]
        acc = jnp.full(idx.shape, NEG_INF, F32) if -1 in buckets else None
        for b in buckets:
            if b < 0:
                continue
            val = tab_ref[b, col0 + h] * scale
            acc = jnp.full(idx.shape, val, F32) if acc is None else jnp.where(idx == b, val, acc)
        o_ref[0, i * rb:(i + 1) * rb, :] = acc


def _build_bias(rel_bias, idx, n_heads, col0, rb=128, scale=1.0):
    r, c = idx.shape
    present = tuple(tuple(int(b) for b in np.unique(idx[i:i + rb])) for i in range(0, r, rb))
    return pl.pallas_call(
        functools.partial(_bias_kernel, col0=col0, scale=scale, rb=rb, present=present),
        grid=(n_heads,),
        in_specs=[pl.BlockSpec(memory_space=pltpu.SMEM),
                  pl.BlockSpec((r, c), lambda h: (0, 0))],
        out_specs=pl.BlockSpec((1, r, c), lambda h: (h, 0, 0)),
        out_shape=jax.ShapeDtypeStruct((n_heads, r, c), F32),
        compiler_params=_cparams(("parallel",)),
        name="bias_tiles",
    )(rel_bias, jnp.asarray(idx))


def _dil_bias_idx():
    row = np.arange(DIL_QBLK)[:, None]
    col = np.arange(DIL_KWIN)[None, :]
    tiles = []
    for d in DILATIONS:
        for off in (0, -DIL_RADIUS, -2 * DIL_RADIUS):
            rel = off + col - row
            idx = _t5_bucket_np(rel * d)
            tiles.append(np.where(np.abs(rel) <= DIL_RADIUS, idx, -1))
    return np.concatenate(tiles, axis=0).astype(np.int32)


def _diff_bias_idx(seq):
    far = np.arange(DIFF_FAR, seq)
    assert np.all(_t5_bucket_np(-far) == NUM_BUCKETS // 2 - 1) and np.all(_t5_bucket_np(far) == NUM_BUCKETS - 1)
    row = np.arange(DIFF_BAND + 2 * DIFF_FAR)[:, None]
    col = np.arange(DIFF_QBLK)[None, :]
    return _t5_bucket_np(row - 2 * DIFF_FAR - col)


def _diff_key_steps(seq):
    key = np.arange(seq)[:, None]
    lane = np.arange(LANES)[None, :]
    step = key >= LANES * (lane & (DIFF_STEPS - 1))
    return np.where((lane < 2 * DIFF_STEPS) & step, 1.0, 0.0).astype(np.float32)


def _dil_kernel(q_ref, k_ref, v_ref, bias_ref, o_ref, m_scr, l_scr, n_scr, *, seq):
    for bi, d in enumerate(DILATIONS):
        sub_len = seq // d
        qblk = sub_len if sub_len == DIL_KWIN else DIL_QBLK
        nblk = sub_len // qblk
        unroll = DIL_UNROLL * DIL_QBLK // qblk
        log_d = d.bit_length() - 1
        first = bi == 0
        last = bi == len(DILATIONS) - 1

        def rows(start, size, d=d):
            if d == 1:
                return pl.ds(pl.multiple_of(start, DIL_QBLK // 2), size)
            return pl.ds(start, size, stride=d)

        def body(grp, carry, d=d, nblk=nblk, log_d=log_d, first=first, last=last, bi=bi,
                 sub_len=sub_len, rows=rows, qblk=qblk, unroll=unroll):
            qsls, scores, vbs = [], [], []
            for u in range(unroll):
                it = grp * unroll + u
                r = it & (d - 1)
                j = it >> log_d
                q0 = j * qblk
                ws = jnp.clip(q0 - DIL_RADIUS, 0, sub_len - DIL_KWIN)
                if nblk == 1:
                    bias = jnp.concatenate([bias_ref[0, bi * 3], bias_ref[0, bi * 3 + 2]], axis=0)
                else:
                    bias = bias_ref[0, bi * 3 + jnp.where(j == 0, 0, jnp.where(j == nblk - 1, 2, 1))]
                qsl = rows(r + d * q0, qblk)
                ksl = rows(r + d * ws, DIL_KWIN)
                qb = q_ref[qsl, :].astype(BF16)
                kb = k_ref[ksl, :].astype(BF16)
                s = lax.dot_general(qb, kb, (((1,), (1,)), ((), ())), preferred_element_type=F32)
                qsls.append(qsl)
                scores.append(s + bias)
                vbs.append(v_ref[ksl, :].astype(BF16))
            stats = []
            for s in scores:
                m = jnp.max(s, axis=1, keepdims=True)
                p = jnp.exp2(s - m)
                stats.append((m, jnp.sum(p, axis=1, keepdims=True), p.astype(BF16)))
            nums = [jnp.dot(p, vb, preferred_element_type=F32) for (_, _, p), vb in zip(stats, vbs)]
            shape = (qblk, LANES)
            for qsl, (m, l, _), n in zip(qsls, stats, nums):
                m = jnp.broadcast_to(m, shape)
                l = jnp.broadcast_to(l, shape)
                if not first:
                    m_old = m_scr[qsl, :]
                    m_new = jnp.maximum(m_old, m)
                    a = jnp.exp2(m_old - m_new)
                    b = jnp.exp2(m - m_new)
                    l = a * l_scr[qsl, :] + b * l
                    n = a * n_scr[qsl, :] + b * n
                    m = m_new
                if last:
                    o_ref[qsl, :] = (n / l).astype(o_ref.dtype)
                else:
                    m_scr[qsl, :] = m
                    l_scr[qsl, :] = l
                    n_scr[qsl, :] = n
            return carry

        lax.fori_loop(0, seq // (qblk * unroll), body, 0)


def _dil_attention(proj, bias_tiles, seq):
    bsz = proj.shape[0]
    nh = N_HEADS_DIL

    def col(off):
        return pl.BlockSpec((None, seq, HEAD_DIM), lambda b, h: (b, 0, off + h))

    state = pltpu.VMEM((seq, LANES), F32)
    return pl.pallas_call(
        functools.partial(_dil_kernel, seq=seq),
        grid=(bsz, nh),
        in_specs=[col(0), col(nh), col(2 * nh),
                  pl.BlockSpec((1, 3 * len(DILATIONS), DIL_QBLK, DIL_KWIN), lambda b, h: (h, 0, 0, 0))],
        out_specs=pl.BlockSpec((None, seq, HEAD_DIM), lambda b, h: (b, 0, h)),
        out_shape=jax.ShapeDtypeStruct((bsz, seq, DIL_WIDTH), BF16),
        scratch_shapes=[state, state, state],
        compiler_params=_cparams(("parallel", "parallel")),
        name="dilated_attention",
    )(proj, proj, proj, bias_tiles)


def _diff_kernel(tab_ref, lam_ref, q_ref, k_ref, ksteps_ref, vt_ref, t_ref, g_ref, o_ref, o_scr,
                 *, seq, lambda_init):
    h = pl.program_id(0)
    n_piece = seq // LANES
    pieces_per_chunk = DIFF_CHUNK // LANES
    chunk_starts = list(range(0, n_piece, pieces_per_chunk))
    width = 2 * DIFF_QBLK

    lam = (jnp.exp(jnp.sum(lam_ref[0:1, :] * lam_ref[1:2, :], axis=1, keepdims=True))
           - jnp.exp(jnp.sum(lam_ref[2:3, :] * lam_ref[3:4, :], axis=1, keepdims=True)) + lambda_init)
    c_left = tab_ref[NUM_BUCKETS // 2 - 1, N_HEADS_DIL + h] * LOG2E
    c_right = tab_ref[NUM_BUCKETS - 1, N_HEADS_DIL + h] * LOG2E
    lane = lax.broadcasted_iota(jnp.int32, (1, LANES), 1)
    lj = lane & (DIFF_STEPS - 1)
    qlane = lax.broadcasted_iota(jnp.int32, (DIFF_QBLK, LANES), 1)

    class SubBlock:
        def __init__(self, sb):
            rows = slice(sb * DIFF_QBLK, (sb + 1) * DIFF_QBLK)
            q0 = (pl.program_id(2) * DIFF_SUBS + sb) * DIFF_QBLK
            ks = jnp.clip(q0 - DIFF_FAR, 0, seq - DIFF_BAND)
            ts = ks - q0 + 2 * DIFF_FAR
            step_lo = lax.shift_right_logical(ks, LANES.bit_length() - 1)
            step_hi = step_lo + DIFF_BAND // LANES
            c = (jnp.where(lj == 0, c_left, 0.0) + jnp.where(lj == step_lo, -c_left, 0.0)
                 + jnp.where(lj == step_hi, c_right, 0.0))
            c = jnp.where(lane < 2 * DIFF_STEPS, c, 0.0)
            c_hi = c.astype(BF16).astype(F32)
            q_extra = jnp.where(lane < DIFF_STEPS, c_hi, c - c_hi).astype(BF16)
            q_extra = jnp.broadcast_to(q_extra, (DIFF_QBLK, LANES))
            q = q_ref[rows, :]
            zero = jnp.zeros_like(q)
            q_both = jnp.concatenate(
                [jnp.concatenate([jnp.where(qlane < DIFF_QK_DIM, q, zero), q_extra], axis=1),
                 jnp.concatenate([jnp.where(qlane >= DIFF_QK_DIM, q, zero), q_extra], axis=1)], axis=0)

            def piece_start(g):
                return pl.multiple_of(((step_lo + g) & (n_piece - 1)) * LANES, LANES)

            def scores(g0):
                k_c = jnp.concatenate(
                    [jnp.concatenate([k_ref[pl.ds(piece_start(g0 + i), LANES), :],
                                      ksteps_ref[pl.ds(piece_start(g0 + i), LANES), :]], axis=1)
                     for i in range(pieces_per_chunk)], axis=0)
                return lax.dot_general(k_c, q_both, (((1,), (1,)), ((), ())), preferred_element_type=F32)

            def band_bias(g0, rows_n=DIFF_CHUNK):
                tile = t_ref[0, pl.ds(pl.multiple_of(ts + g0 * LANES, LANES), rows_n), :]
                return jnp.concatenate([tile, tile], axis=1)

            def values_t(g0):
                return jnp.concatenate([vt_ref[:, pl.ds(piece_start(g0 + i), LANES)]
                                        for i in range(pieces_per_chunk)], axis=1)

            def finish(o, l):
                on = o / l
                od = on[:, :DIFF_QBLK] - lam * on[:, DIFF_QBLK:]
                ms = jnp.mean(od * od, axis=0, keepdims=True)
                y = od * lax.rsqrt(ms + SUBLN_EPS) * g_ref[...] * (1.0 - lambda_init)
                o_ref[rows, :] = y.T.astype(o_ref.dtype)

            self.scores, self.band_bias, self.values_t, self.finish = scores, band_bias, values_t, finish

    subs = [SubBlock(sb) for sb in range(DIFF_SUBS)]
    results = []
    s_next = subs[0].scores(0)
    for sb, sub in enumerate(subs):
        s = s_next
        ref = jnp.max(s[:LANES] + sub.band_bias(0, LANES), axis=0, keepdims=True)
        l8 = jnp.zeros((8, width), F32)
        o = None
        for ci, g0 in enumerate(chunk_starts):
            if ci + 1 < len(chunk_starts):
                s_next = sub.scores(chunk_starts[ci + 1])
            elif sb + 1 < len(subs):
                s_next = subs[sb + 1].scores(0)
            if g0 * LANES < DIFF_BAND:
                s = s + sub.band_bias(g0)
            p = jnp.exp2(s - ref)
            l8 = l8 + jnp.sum(p.reshape(DIFF_CHUNK // 8, 8, width), axis=0)
            o_c = jnp.dot(sub.values_t(g0), p.astype(BF16), preferred_element_type=F32)
            o = o_c if o is None else o + o_c
            s = s_next
        l = jnp.sum(l8, axis=0, keepdims=True)
        sub.finish(o, l)
        results.append((o, l))

    def all_finite(x):
        return jnp.min(jnp.where(jnp.isfinite(x), 1.0, 0.0))

    for sub, (o, l) in zip(subs, results):
        @pl.when(jnp.logical_or(jnp.minimum(all_finite(o), all_finite(l)) < 0.5, jnp.max(l) > DIFF_MAX_DENOM))
        def _(sub=sub):
            def chunk(c, carry, in_band):
                m_old, l_old = carry
                g0 = c * pieces_per_chunk
                sc = sub.scores(g0)
                if in_band:
                    sc = sc + sub.band_bias(g0)
                m_new = jnp.maximum(m_old, jnp.max(sc, axis=0, keepdims=True))
                alpha = jnp.exp2(m_old - m_new)
                pc = jnp.exp2(sc - m_new)
                o_scr[...] = alpha * o_scr[...] + jnp.dot(sub.values_t(g0), pc.astype(BF16),
                                                          preferred_element_type=F32)
                return m_new, alpha * l_old + jnp.sum(pc, axis=0, keepdims=True)

            o_scr[...] = jnp.zeros_like(o_scr)
            carry = (jnp.full((1, width), NEG_INF, F32), jnp.zeros((1, width), F32))
            n_band = DIFF_BAND // DIFF_CHUNK
            carry = lax.fori_loop(0, n_band, functools.partial(chunk, in_band=True), carry)
            _, l_slow = lax.fori_loop(n_band, len(chunk_starts), functools.partial(chunk, in_band=False), carry)
            sub.finish(o_scr[...], l_slow)


def _diff_attention(rel_bias, proj, vt, t_bias, lam_vecs, g_subln, seq, lambda_init):
    bsz = proj.shape[0]
    nh = N_HEADS_DIFF
    tq = DIFF_SUBS * DIFF_QBLK
    nq = seq // tq
    assert seq == DIFF_STEPS * LANES and 2 * DIFF_STEPS <= LANES
    assert DIFF_BAND % DIFF_CHUNK == 0 and seq % DIFF_CHUNK == 0 and seq % tq == 0
    ksteps = jnp.asarray(_diff_key_steps(seq), BF16)
    return pl.pallas_call(
        functools.partial(_diff_kernel, seq=seq, lambda_init=lambda_init),
        grid=(nh, bsz, nq),
        in_specs=[pl.BlockSpec(memory_space=pltpu.SMEM),
                  pl.BlockSpec((4, DIFF_QK_DIM), lambda h, b, i: (0, 0)),
                  pl.BlockSpec((None, tq, HEAD_DIM), lambda h, b, i: (b, i, h)),
                  pl.BlockSpec((None, seq, HEAD_DIM), lambda h, b, i: (b, 0, nh + h)),
                  pl.BlockSpec((seq, LANES), lambda h, b, i: (0, 0)),
                  pl.BlockSpec((None, None, HEAD_DIM, seq), lambda h, b, i: (b, h, 0, 0)),
                  pl.BlockSpec((1, DIFF_BAND + 2 * DIFF_FAR, DIFF_QBLK), lambda h, b, i: (h, 0, 0)),
                  pl.BlockSpec((HEAD_DIM, 1), lambda h, b, i: (0, 0))],
        out_specs=pl.BlockSpec((None, tq, HEAD_DIM), lambda h, b, i: (b, i, h)),
        out_shape=jax.ShapeDtypeStruct((bsz, seq, DIFF_WIDTH), BF16),
        scratch_shapes=[pltpu.VMEM((HEAD_DIM, 2 * DIFF_QBLK), F32)],
        compiler_params=_cparams(("parallel", "parallel", "parallel")),
        name="diff_attention",
    )(rel_bias, lam_vecs, proj, proj, ksteps, vt, t_bias, g_subln.reshape(HEAD_DIM, 1))


def kernel(x, rel_bias, g_pre_mix, w_in, lambda_q1, lambda_k1, lambda_q2, lambda_k2, g_subln, w_out,
           g_post_mix, g_pre_mlp, w_up, w_down, g_post_mlp):
    bsz, seq, d = x.shape
    m = bsz * seq
    xf = x.reshape(m, d)

    dil_tiles = _build_bias(rel_bias, _dil_bias_idx(), N_HEADS_DIL, 0, scale=LOG2E)
    dil_tiles = dil_tiles.reshape(N_HEADS_DIL, 3 * len(DILATIONS), DIL_QBLK, DIL_KWIN)
    diff_t = _build_bias(rel_bias, _diff_bias_idx(seq), N_HEADS_DIFF, N_HEADS_DIL, scale=LOG2E)

    ones = jnp.ones((DIL_WIDTH,), F32)
    scale_dil = jnp.concatenate([ones * (HEAD_DIM ** -0.5 * LOG2E), ones, ones])
    scale_diff = jnp.concatenate([ones * (DIFF_QK_DIM ** -0.5 * LOG2E), ones, ones])

    h = _rmsnorm(xf, g_pre_mix[0])
    for l in range(DEPTH):
        lambda_init = 0.8 - 0.6 * math.exp(-0.3 * l)
        proj_a = _matmul_wres(h, w_in, l, F32, colscale=scale_dil, b_cols=(0, 3 * DIL_WIDTH))
        proj_d = _matmul(h, w_in, l, BF16, colscale=scale_diff, b_cols=(3 * DIL_WIDTH, 3 * DIFF_WIDTH), tk=2048)
        out_a = _dil_attention(proj_a.reshape(bsz, seq, 3 * DIL_WIDTH), dil_tiles, seq)
        lam_vecs = jnp.stack([lambda_q1[l], lambda_k1[l], lambda_q2[l], lambda_k2[l]]).astype(F32)
        proj_d = proj_d.reshape(bsz, seq, 3 * DIFF_WIDTH)
        vt = proj_d[:, :, 2 * DIFF_WIDTH:].reshape(bsz, seq, N_HEADS_DIFF, HEAD_DIM).transpose(0, 2, 3, 1)
        out_d = _diff_attention(rel_bias, proj_d, vt, diff_t, lam_vecs, g_subln[l], seq, lambda_init)
        y = _matmul((out_a.reshape(m, DIL_WIDTH), out_d.reshape(m, DIFF_WIDTH)), w_out, l, BF16,
                    tm=1024, tk=DIL_WIDTH)
        xf, h = _norm_res(xf, y, g_post_mix[l], g_pre_mlp[l])
        u = _matmul(h, w_up, l, BF16, act="relu2", tk=2048)
        y = _matmul(u, w_down, l, BF16, tk=2048)
        if l + 1 < DEPTH:
            xf, h = _norm_res(xf, y, g_post_mlp[l], g_pre_mix[l + 1])
        else:
            xf = _norm_res(xf, y, g_post_mlp[l])
    return xf.reshape(bsz, seq, d)
```
